```python
import math
import jax, jax.numpy as jnp
from jax import lax
import numpy as np

D_MODEL = 4096
BATCH = 16
SEQ = 256
DEPTH = 4
DEC_BATCH = 8
DEC_SEQ = 1024
PAST_LEN = 512

GRID_W = 64
N_MIXERS = 2
N_SSM_LAYERS = (DEPTH + 1) // 2
N_MLP_LAYERS = DEPTH // 2
D_SSM = D_MODEL
GROUP_CH = 16
N_SSM_GROUPS = D_SSM // GROUP_CH
STATE_N = 64
SCAN_CHUNK = 128
D_GMLP = D_MODEL
GMLP_HEADS = 16
GMLP_HEAD_DIM = D_GMLP // GMLP_HEADS
GMLP_CHUNK = 128
N_EXPERTS = 16
N_EXPERT_GROUPS = 4
EXPERTS_PER_GROUP = N_EXPERTS // N_EXPERT_GROUPS
TOP_K = 2
D_EXPERT = D_MODEL // 4
ALPHA = (2 * DEPTH) ** 0.25
BETA = (8 * DEPTH) ** -0.25
LN_EPS = 1e-5
N_MOD = 6

kernel_name = "hybrid_s5_gmlp_moe_diffusion_step"


def _layernorm(x, g, b):
    xf = x.astype(jnp.float32)
    mu = jnp.mean(xf, axis=-1, keepdims=True)
    var = jnp.mean(jnp.square(xf - mu), axis=-1, keepdims=True)
    y = (xf - mu) * lax.rsqrt(var + LN_EPS) * g.astype(jnp.float32) + b.astype(jnp.float32)
    return y.astype(x.dtype)


def _pos2d(length, dtype):
    rows = length // GRID_W
    r = jnp.repeat(jnp.arange(rows, dtype=jnp.float32), GRID_W)
    col = jnp.tile(jnp.arange(GRID_W, dtype=jnp.float32), rows)
    nf = D_MODEL // 4
    omega = 1.0 / (10000.0 ** (jnp.arange(nf, dtype=jnp.float32) / nf))

    def enc(p):
        a = p[:, None] * omega[None, :]
        return jnp.concatenate([jnp.sin(a), jnp.cos(a)], axis=-1)

    return jnp.concatenate([enc(r), enc(col)], axis=-1).astype(dtype)


def _adaln(cond_silu, w, b):
    m = cond_silu.astype(w.dtype) @ w + b
    return m.reshape(cond_silu.shape[0], N_MOD, D_MODEL)


def _modulate(x, shift, scale):
    return x * (1 + scale[:, None, :]) + shift[:, None, :]


def _post_norm(x, delta, g, b):
    return _layernorm(ALPHA * x + delta, g, b)


def _zoh(lam_re, lam_im, log_dt, b_re, b_im):
    dt = jnp.exp(log_dt)[:, None]
    mag = jnp.exp(lam_re * dt)
    ar = mag * jnp.cos(lam_im * dt)
    ai = mag * jnp.sin(lam_im * dt)
    den = lam_re * lam_re + lam_im * lam_im
    cr = ((ar - 1) * lam_re + ai * lam_im) / den
    ci = (ai * lam_re - (ar - 1) * lam_im) / den
    bbar_r = cr[..., None] * b_re - ci[..., None] * b_im
    bbar_i = cr[..., None] * b_im + ci[..., None] * b_re
    return ar, ai, bbar_r, bbar_i


def _combine(e1, e2):
    a1r, a1i, b1r, b1i = e1
    a2r, a2i, b2r, b2i = e2
    return (a2r * a1r - a2i * a1i,
            a2r * a1i + a2i * a1r,
            a2r * b1r - a2i * b1i + b2r,
            a2r * b1i + a2i * b1r + b2i)


def _diag_scan(u, h0_r, h0_i, abar_r, abar_i, bbar_r, bbar_i, c_r, c_i):
    bsz, length = u.shape[0], u.shape[1]
    n_chunks = length // SCAN_CHUNK
    uc = u.reshape(bsz, n_chunks, SCAN_CHUNK, N_SSM_GROUPS, GROUP_CH).swapaxes(0, 1)

    def step(carry, u_blk):
        hr, hi = carry
        br = jnp.einsum('btgc,gnc->btgn', u_blk, bbar_r)
        bi = jnp.einsum('btgc,gnc->btgn', u_blk, bbar_i)
        ar = jnp.broadcast_to(abar_r, br.shape)
        ai = jnp.broadcast_to(abar_i, br.shape)
        pr, pim, sr, si = lax.associative_scan(_combine, (ar, ai, br, bi), axis=1)
        hr_t = pr * hr[:, None] - pim * hi[:, None] + sr
        hi_t = pr * hi[:, None] + pim * hr[:, None] + si
        y = (jnp.einsum('btgn,gcn->btgc', hr_t, c_r)
             - jnp.einsum('btgn,gcn->btgc', hi_t, c_i))
        return (hr_t[:, -1], hi_t[:, -1]), y

    (hr, hi), ys = lax.scan(step, (h0_r, h0_i), uc)
    y = ys.swapaxes(0, 1).reshape(bsz, length, N_SSM_GROUPS, GROUP_CH)
    return y, hr, hi


def _s5_mixer(h, h0_r, h0_i, w_in, lam_re, lam_im, log_dt, b_re, b_im, c_re, c_im,
              d_skip, w_glu, w_out):
    f32 = jnp.float32
    bsz, length, _ = h.shape
    u = (h @ w_in).astype(f32)
    ug = u.reshape(bsz, length, N_SSM_GROUPS, GROUP_CH)
    y = d_skip.astype(f32) * u
    fin_r, fin_i = [], []
    for k in range(2):
        abar_r, abar_i, bbar_r, bbar_i = _zoh(lam_re[k].astype(f32), lam_im[k].astype(f32),
                                              log_dt[k].astype(f32), b_re[k].astype(f32),
                                              b_im[k].astype(f32))
        seq = ug if k == 0 else jnp.flip(ug, axis=1)
        yk, fr, fi = _diag_scan(seq, h0_r[:, k].astype(f32), h0_i[:, k].astype(f32),
                                abar_r, abar_i, bbar_r, bbar_i,
                                c_re[k].astype(f32), c_im[k].astype(f32))
        if k == 1:
            yk = jnp.flip(yk, axis=1)
        y = y + yk.reshape(bsz, length, D_SSM)
        fin_r.append(fr)
        fin_i.append(fi)
    z = jax.nn.gelu(y)
    z = z * jax.nn.sigmoid(z @ w_glu.astype(f32))
    out = (z @ w_out.astype(f32)).astype(h.dtype)
    return out, jnp.stack(fin_r, axis=1), jnp.stack(fin_i, axis=1)


def _gmlp_mixer(h, w_in, b_in, g_ln, b_ln, w_s, b_s, w_out):
    bsz, length, _ = h.shape
    z = jax.nn.gelu(h @ w_in + b_in)
    u, v = jnp.split(z, 2, axis=-1)
    v = _layernorm(v, g_ln, b_ln)
    vb = v.reshape(bsz, length // GMLP_CHUNK, GMLP_CHUNK, GMLP_HEADS, GMLP_HEAD_DIM)
    s = jnp.einsum('hpq,bnqhd->bnphd', w_s, vb) + b_s.T[:, :, None]
    return (u * s.reshape(bsz, length, D_GMLP)) @ w_out


def _moe(h, w_router, b_router, w_gate, w_up, w_down):
    f32 = jnp.float32
    probs = jax.nn.softmax((h @ w_router).astype(f32), axis=-1)
    sel = probs + b_router.astype(f32)
    sel_g = sel.reshape(sel.shape[:-1] + (N_EXPERT_GROUPS, EXPERTS_PER_GROUP))
    group_score = jnp.sum(lax.top_k(sel_g, TOP_K)[0], axis=-1)
    best = jnp.argmax(group_score, axis=-1)
    in_group = (jnp.arange(N_EXPERT_GROUPS) == best[..., None])[..., None]
    masked = jnp.where(in_group, sel_g, -jnp.inf).reshape(sel.shape)
    _, idx = lax.top_k(masked, TOP_K)
    gates = jnp.take_along_axis(probs, idx, axis=-1)
    gates = gates / jnp.sum(gates, axis=-1, keepdims=True)
    combine = jnp.einsum('blk,blke->ble', gates,
                         jax.nn.one_hot(idx, N_EXPERTS, dtype=f32)).astype(h.dtype)
    out = jnp.zeros_like(h)
    for e in range(N_EXPERTS):
        y_e = (jax.nn.silu(h @ w_gate[e]) * (h @ w_up[e])) @ w_down[e]
        out = out + combine[..., e:e + 1] * y_e
    return out


def setup_inputs(seed: int = 0) -> dict:
    key = jax.random.key(seed)
    ks = iter(jax.random.split(key, 48))
    f32 = jnp.float32

    def nrm(shape, scale):
        return scale * jax.random.normal(next(ks), shape, f32)

    NS, NG = N_SSM_LAYERS, N_MLP_LAYERS
    lam_im_base = jnp.pi * jnp.arange(STATE_N, dtype=f32)
    return {
        "x_prompt": nrm((BATCH, SEQ, D_MODEL), 1.0),
        "x_sample": nrm((DEC_BATCH, DEC_SEQ, D_MODEL), 1.0),
        "state_ssm_re": nrm((DEC_BATCH, NS, 2, N_SSM_GROUPS, STATE_N), 0.3),
        "state_ssm_im": nrm((DEC_BATCH, NS, 2, N_SSM_GROUPS, STATE_N), 0.3),
        "c": nrm((DEC_BATCH, D_MODEL), 1.0),
        "c_ctx": nrm((D_MODEL,), 1.0),
        "w_ada": nrm((DEPTH, D_MODEL, N_MOD * D_MODEL), 0.5 * D_MODEL ** -0.5),
        "b_ada": nrm((DEPTH, N_MOD * D_MODEL), 0.02),
        "ln_g": 1.0 + nrm((DEPTH, 2, D_MODEL), 0.02),
        "ln_b": nrm((DEPTH, 2, D_MODEL), 0.02),
        "s5_w_in": nrm((NS, D_MODEL, D_SSM), D_MODEL ** -0.5),
        "s5_lam_re": -0.5 + nrm((NS, 2, N_SSM_GROUPS, STATE_N), 0.01),
        "s5_lam_im": lam_im_base + nrm((NS, 2, N_SSM_GROUPS, STATE_N), 0.01),
        "s5_log_dt": jax.random.uniform(next(ks), (NS, 2, N_SSM_GROUPS), f32,
                                        math.log(1e-3), math.log(1e-1)),
        "s5_b_re": nrm((NS, 2, N_SSM_GROUPS, STATE_N, GROUP_CH), (0.5 / GROUP_CH) ** 0.5),
        "s5_b_im": nrm((NS, 2, N_SSM_GROUPS, STATE_N, GROUP_CH), (0.5 / GROUP_CH) ** 0.5),
        "s5_c_re": nrm((NS, 2, N_SSM_GROUPS, GROUP_CH, STATE_N), (0.5 / STATE_N) ** 0.5),
        "s5_c_im": nrm((NS, 2, N_SSM_GROUPS, GROUP_CH, STATE_N), (0.5 / STATE_N) ** 0.5),
        "s5_d": nrm((NS, D_SSM), 1.0),
        "s5_w_glu": nrm((NS, D_SSM, D_SSM), D_SSM ** -0.5),
        "s5_w_out": nrm((NS, D_SSM, D_MODEL), BETA * D_SSM ** -0.5),
        "g_w_in": nrm((NG, D_MODEL, 2 * D_GMLP), D_MODEL ** -0.5),
        "g_b_in": nrm((NG, 2 * D_GMLP), 0.02),
        "g_ln_g": 1.0 + nrm((NG, D_GMLP), 0.02),
        "g_ln_b": nrm((NG, D_GMLP), 0.02),
        "g_w_s": nrm((NG, GMLP_HEADS, GMLP_CHUNK, GMLP_CHUNK), GMLP_CHUNK ** -0.5),
        "g_b_s": 1.0 + nrm((NG, GMLP_HEADS, GMLP_CHUNK), 0.02),
        "g_w_out": nrm((NG, D_GMLP, D_MODEL), BETA * D_GMLP ** -0.5),
        "w_router": nrm((D_MODEL, N_EXPERTS), D_MODEL ** -0.5),
        "b_router": nrm((N_EXPERTS,), 0.01),
        "moe_w_gate": nrm((DEPTH, N_EXPERTS, D_MODEL, D_EXPERT), D_MODEL ** -0.5),
        "moe_w_up": nrm((DEPTH, N_EXPERTS, D_MODEL, D_EXPERT), D_MODEL ** -0.5),
        "moe_w_down": nrm((DEPTH, N_EXPERTS, D_EXPERT, D_MODEL), BETA * D_EXPERT ** -0.5),
    }


def reference(x_prompt, x_sample, state_ssm_re, state_ssm_im, c, c_ctx, w_ada, b_ada,
              ln_g, ln_b, s5_w_in, s5_lam_re, s5_lam_im, s5_log_dt, s5_b_re, s5_b_im,
              s5_c_re, s5_c_im, s5_d, s5_w_glu, s5_w_out, g_w_in, g_b_in, g_ln_g, g_ln_b,
              g_w_s, g_b_s, g_w_out, w_router, b_router, moe_w_gate, moe_w_up, moe_w_down):
    f32 = jnp.float32
    xp = x_prompt
    xs = x_sample + _pos2d(x_sample.shape[1], x_sample.dtype)[None]
    s_ctx = jax.nn.silu(c_ctx.astype(f32))[None]
    s_lat = jax.nn.silu(c.astype(f32))
    zeros_h = jnp.zeros((xp.shape[0], 2, N_SSM_GROUPS, STATE_N), f32)
    new_re, new_im = [], []
    for i in range(DEPTH):
        mp = _adaln(s_ctx, w_ada[i], b_ada[i])
        ms = _adaln(s_lat, w_ada[i], b_ada[i])
        hp = _modulate(xp, mp[:, 0], mp[:, 1])
        hs = _modulate(xs, ms[:, 0], ms[:, 1])
        j = i // N_MIXERS
        if i % N_MIXERS == 0:
            prm = (s5_w_in[j], s5_lam_re[j], s5_lam_im[j], s5_log_dt[j], s5_b_re[j],
                   s5_b_im[j], s5_c_re[j], s5_c_im[j], s5_d[j], s5_w_glu[j], s5_w_out[j])
            yp, fr, fi = _s5_mixer(hp, zeros_h, zeros_h, *prm)
            ys, _, _ = _s5_mixer(hs, state_ssm_re[:, j], state_ssm_im[:, j], *prm)
            new_re.append(fr.astype(x_prompt.dtype))
            new_im.append(fi.astype(x_prompt.dtype))
        else:
            prm = (g_w_in[j], g_b_in[j], g_ln_g[j], g_ln_b[j], g_w_s[j], g_b_s[j], g_w_out[j])
            yp = _gmlp_mixer(hp, *prm)
            ys = _gmlp_mixer(hs, *prm)
        xp = _post_norm(xp, mp[:, 2][:, None] * yp, ln_g[i, 0], ln_b[i, 0])
        xs = _post_norm(xs, ms[:, 2][:, None] * ys, ln_g[i, 0], ln_b[i, 0])
        hp = _modulate(xp, mp[:, 3], mp[:, 4])
        hs = _modulate(xs, ms[:, 3], ms[:, 4])
        fp = _moe(hp, w_router, b_router, moe_w_gate[i], moe_w_up[i], moe_w_down[i])
        fs = _moe(hs, w_router, b_router, moe_w_gate[i], moe_w_up[i], moe_w_down[i])
        xp = _post_norm(xp, mp[:, 5][:, None] * fp, ln_g[i, 1], ln_b[i, 1])
        xs = _post_norm(xs, ms[:, 5][:, None] * fs, ln_g[i, 1], ln_b[i, 1])
    new_state_ssm_re = jnp.stack(new_re, axis=1)
    new_state_ssm_im = jnp.stack(new_im, axis=1)
    return (xp, xs, new_state_ssm_re, new_state_ssm_im)
```

```python
import functools

import jax
import jax.numpy as jnp
from jax import lax
from jax.experimental import pallas as pl
from jax.experimental.pallas import tpu as pltpu

F32, BF16, U32, I32 = jnp.float32, jnp.bfloat16, jnp.uint32, jnp.int32

LN_EPS = 1e-5
GRID_W = 64
N_MOD = 6
N_EXPERT_GROUPS = 4
COND_ROWS = 16
LANES = 128
SUBLANES = 8
CHUNK = SUBLANES
PAIR = 2 * LANES
V7X_VMEM_LIMIT = 56 * 2**20
HIGHEST = lax.Precision.HIGHEST


def _params(sem, vmem=V7X_VMEM_LIMIT):
    return pltpu.CompilerParams(dimension_semantics=sem, vmem_limit_bytes=vmem)


def _cond_row(tile, tm, tp, ls):
    start = tile * tm
    return jnp.where(start < tp, 0, 1 + jnp.maximum(start - tp, 0) // ls)


def _ln(x, g, b):
    mu = jnp.mean(x, axis=-1, keepdims=True)
    xc = x - mu
    var = jnp.mean(xc * xc, axis=-1, keepdims=True)
    return xc * lax.rsqrt(var + LN_EPS) * g + b


def _pitch(nchunks):
    p = -(-nchunks // SUBLANES) * SUBLANES
    if (p // SUBLANES) % 2 == 0:
        p += SUBLANES
    return p


def _pack_pair(a, b):
    ua = lax.bitcast_convert_type(a.astype(BF16).astype(F32), U32)
    ub = lax.bitcast_convert_type(b.astype(BF16).astype(F32), U32)
    return (ua >> 16) | ub


def _unpack_pair(w):
    lo = lax.bitcast_convert_type(w << 16, F32)
    hi = lax.bitcast_convert_type(w & jnp.uint32(0xFFFF0000), F32)
    return lo, hi


def _store_packed(ref, h, rows, pitch):
    nch = h.shape[1] // PAIR
    for s in range(nch):
        ref[pl.ds(s, rows, stride=pitch), :] = _pack_pair(
            h[:, s * PAIR:s * PAIR + LANES], h[:, s * PAIR + LANES:(s + 1) * PAIR])
    for s in range(nch, pitch):
        ref[pl.ds(s, rows, stride=pitch), :] = jnp.zeros((rows, LANES), U32)


def _adaln_kernel(c_ref, w_ref, b_ref, o_ref):
    s = jax.nn.silu(c_ref[...]).astype(BF16)
    o_ref[...] = jnp.dot(s, w_ref[...].astype(BF16), preferred_element_type=F32) + b_ref[...]


def _adaln(cond, w_ada, b_ada):
    depth, d, nm = w_ada.shape
    tn = min(512, nm)
    return pl.pallas_call(
        _adaln_kernel,
        out_shape=jax.ShapeDtypeStruct((depth, COND_ROWS, nm), F32),
        grid=(depth, nm // tn),
        in_specs=[pl.BlockSpec((COND_ROWS, d), lambda i, n: (0, 0)),
                  pl.BlockSpec((None, d, tn), lambda i, n: (i, 0, n)),
                  pl.BlockSpec((None, 1, tn), lambda i, n: (i, 0, n))],
        out_specs=pl.BlockSpec((None, COND_ROWS, tn), lambda i, n: (i, 0, n)),
        compiler_params=_params(("arbitrary", "arbitrary")),
        name="adaln",
    )(cond, w_ada, b_ada.reshape(depth, 1, nm))


def _mod_spec(d, layer, m):
    return pl.BlockSpec((None, COND_ROWS, d), lambda *_: (layer, 0, m))


def _embed_kernel(xp_ref, xs_ref, pos_ref, sh_ref, sc_ref, x_ref, h_ref, *, tm, tp, ls):
    i = pl.program_id(0)
    r = _cond_row(i, tm, tp, ls)
    sh = sh_ref[pl.ds(r, 1), :]
    sc = sc_ref[pl.ds(r, 1), :]

    def emit(x):
        x_ref[...] = x
        h_ref[...] = (x * (1 + sc) + sh).astype(BF16)

    @pl.when(i * tm < tp)
    def _():
        emit(xp_ref[...])

    @pl.when(i * tm >= tp)
    def _():
        emit(xs_ref[...] + pos_ref[...])


def _embed(xp, xs, pos, mods, tm, ls):
    tp, d = xp.shape
    ts = xs.shape[0]
    npt, t = tp // tm, tp + ts
    nps = ls // tm
    return pl.pallas_call(
        functools.partial(_embed_kernel, tm=tm, tp=tp, ls=ls),
        out_shape=(jax.ShapeDtypeStruct((t, d), F32), jax.ShapeDtypeStruct((t, d), BF16)),
        grid=(t // tm,),
        in_specs=[pl.BlockSpec((tm, d), lambda i: (jnp.minimum(i, npt - 1), 0)),
                  pl.BlockSpec((tm, d), lambda i: (jnp.maximum(i - npt, 0), 0)),
                  pl.BlockSpec((tm, d), lambda i: (jnp.maximum(i - npt, 0) % nps, 0)),
                  _mod_spec(d, 0, 0), _mod_spec(d, 0, 1)],
        out_specs=(pl.BlockSpec((tm, d), lambda i: (i, 0)), pl.BlockSpec((tm, d), lambda i: (i, 0))),
        compiler_params=_params(("arbitrary",)),
        name="embed",
    )(xp, xs, pos, mods, mods)


def _epi_store(acc, extra, outs):
    outs[0][...] = acc.astype(outs[0].dtype)


def _epi_bias_gelu(acc, extra, outs):
    outs[0][...] = jax.nn.gelu(acc + extra[0][...]).astype(outs[0].dtype)


def _epi_glu(acc, extra, outs):
    outs[0][...] = (extra[0][...].astype(F32) * jax.nn.sigmoid(acc)).astype(outs[0].dtype)


def _mm_kernel(*refs, nk, n_extra, epilogue):
    x_ref, w_ref = refs[:2]
    extra = refs[2:2 + n_extra]
    part = jnp.dot(x_ref[...].astype(BF16), w_ref[...].astype(BF16), preferred_element_type=F32)
    if nk == 1:
        epilogue(part, extra, refs[2 + n_extra:])
        return
    outs, acc = refs[2 + n_extra:-1], refs[-1]
    k = pl.program_id(2)

    @pl.when(k == 0)
    def _():
        acc[...] = part

    @pl.when(k > 0)
    def _():
        acc[...] += part

    @pl.when(k == nk - 1)
    def _():
        epilogue(acc[...], extra, outs)


def _matmul(x, w, w_prefix, *, epilogue=_epi_store, extra=(), extra_specs=(), out_dtype=F32, name):
    m, kdim = x.shape
    n = w.shape[-1]
    tm, tn, tk = min(1024, m), min(1024, n), min(2048, kdim)
    nk = kdim // tk
    npre = len(w_prefix)
    return pl.pallas_call(
        functools.partial(_mm_kernel, nk=nk, n_extra=len(extra), epilogue=epilogue),
        out_shape=jax.ShapeDtypeStruct((m, n), out_dtype),
        grid=(m // tm, n // tn, nk),
        in_specs=[pl.BlockSpec((tm, tk), lambda i, j, k: (i, k)),
                  pl.BlockSpec((None,) * npre + (tk, tn), lambda i, j, k: tuple(w_prefix) + (k, j)),
                  *extra_specs],
        out_specs=pl.BlockSpec((tm, tn), lambda i, j, k: (i, j)),
        scratch_shapes=[pltpu.VMEM((tm, tn), F32)] if nk > 1 else [],
        compiler_params=_params(("arbitrary", "arbitrary", "arbitrary")),
        name=name,
    )(x, w, *extra), (tm, tn)


def _s5_prep_kernel(lre_ref, lim_ref, ldt_ref, btr_ref, bti_ref, ctr_ref, cti_ref,
                    wall_ref, qt_ref, a8_ref, *, group_ch, state_n):
    half = lre_ref.shape[-1]
    kc = CHUNK * LANES
    mask = (lax.broadcasted_iota(I32, (LANES, half), 0) // group_ch
            == lax.broadcasted_iota(I32, (LANES, half), 1) // state_n)
    tstacks, a8rows = [], []
    for k in range(2):
        lr, li = lre_ref[k], lim_ref[k]
        dt = jnp.exp(ldt_ref[k])
        mag = jnp.exp(lr * dt)
        ar, ai = mag * jnp.cos(li * dt), mag * jnp.sin(li * dt)
        den = lr * lr + li * li
        cr = ((ar - 1) * lr + ai * li) / den
        ci = (ai * lr - (ar - 1) * li) / den
        pw = [(jnp.ones_like(ar), jnp.zeros_like(ar))]
        for _ in range(CHUNK):
            pr, pi = pw[-1]
            pw.append((pr * ar - pi * ai, pr * ai + pi * ar))
        br, bi = btr_ref[k], bti_ref[k]
        crr, cii = ctr_ref[k], cti_ref[k]
        ptiles = []
        for s in range(CHUNK):
            e = CHUNK - 1 - s if k == 0 else s
            zr = cr * pw[e][0] - ci * pw[e][1]
            zi = cr * pw[e][1] + ci * pw[e][0]
            tile = jnp.concatenate([jnp.where(mask, zr * br - zi * bi, 0.0),
                                    jnp.where(mask, zr * bi + zi * br, 0.0)], axis=1)
            ptiles.append(tile)
            wall_ref[s * LANES:(s + 1) * LANES, (1 + k) * kc:(2 + k) * kc] = tile.astype(BF16)
        qt0 = jnp.concatenate([jnp.where(mask, crr, 0.0), jnp.where(mask, -cii, 0.0)], axis=1)
        tstacks.append(lax.dot_general(jnp.concatenate(ptiles, axis=0), qt0, (((1,), (1,)), ((), ())),
                                       precision=HIGHEST, preferred_element_type=F32))
        for t in range(CHUNK):
            e = t + 1 if k == 0 else CHUNK - t
            tile = jnp.concatenate([jnp.where(mask, pw[e][0] * crr - pw[e][1] * cii, 0.0),
                                    jnp.where(mask, -(pw[e][0] * cii + pw[e][1] * crr), 0.0)], axis=1)
            qt_ref[t * LANES:(t + 1) * LANES, k * kc:(k + 1) * kc] = tile.astype(BF16)
        a8rows += [pw[CHUNK][0], pw[CHUNK][1]]
    tf, tb = tstacks
    for s in range(CHUNK):
        for t in range(CHUNK):
            if t > s:
                blk = CHUNK - 1 - (t - s)
                tile = tf[blk * LANES:(blk + 1) * LANES]
            elif s > t:
                tile = tb[(s - t) * LANES:(s - t + 1) * LANES]
            else:
                tile = tf[(CHUNK - 1) * LANES:] + tb[:LANES]
            wall_ref[s * LANES:(s + 1) * LANES, t * LANES:(t + 1) * LANES] = tile.astype(BF16)
    a8_ref[...] = jnp.concatenate(a8rows + [jnp.zeros_like(a8rows[0])] * (SUBLANES - 4), axis=0)


def _s5_prep(lam_re, lam_im, log_dt, b_re, b_im, c_re, c_im):
    _, g, n = lam_re.shape
    gc = b_re.shape[-1]
    gps = LANES // gc
    nslab, half = g // gps, gps * n
    kc = CHUNK * LANES

    def lane(a):
        return a.reshape(2, nslab, 1, half)

    def rows(a):
        return jnp.tile(a.reshape(2, nslab, LANES, n), (1, 1, 1, gps))

    ins = (lane(lam_re), lane(lam_im), lane(jnp.broadcast_to(log_dt[..., None], lam_re.shape)),
           rows(jnp.swapaxes(b_re, -1, -2)), rows(jnp.swapaxes(b_im, -1, -2)), rows(c_re), rows(c_im))
    lspec = pl.BlockSpec((2, None, 1, half), lambda c: (0, c, 0, 0))
    rspec = pl.BlockSpec((2, None, LANES, half), lambda c: (0, c, 0, 0))
    return pl.pallas_call(
        functools.partial(_s5_prep_kernel, group_ch=gc, state_n=n),
        out_shape=(jax.ShapeDtypeStruct((nslab, kc, 3 * kc), BF16),
                   jax.ShapeDtypeStruct((nslab, kc, 2 * kc), BF16),
                   jax.ShapeDtypeStruct((nslab, SUBLANES, half), F32)),
        grid=(nslab,),
        in_specs=[lspec, lspec, lspec, rspec, rspec, rspec, rspec],
        out_specs=(pl.BlockSpec((None, kc, 3 * kc), lambda c: (c, 0, 0)),
                   pl.BlockSpec((None, kc, 2 * kc), lambda c: (c, 0, 0)),
                   pl.BlockSpec((None, SUBLANES, half), lambda c: (c, 0, 0))),
        compiler_params=_params(("arbitrary",)),
        name="s5_prep",
    )(*ins)


def _s5_kernel(u_ref, d_ref, wall_ref, qt_ref, a8_ref, h0_ref, z_ref, fin_ref,
               xf_s, xb_s, sf_s, sb_s, zs_s, *, r, npp, nbp, mp, nbs, ms):
    p = pl.program_id(1)
    rows = r // CHUNK
    kc = CHUNK * LANES
    half = a8_ref.shape[-1]
    us = [u_ref[pl.ds(s, rows, stride=CHUNK), :] for s in range(CHUNK)]
    lhs = jnp.concatenate([x.astype(BF16) for x in us], axis=1)
    big = jnp.dot(lhs, wall_ref[...], preferred_element_type=F32)
    a8 = a8_ref[...]
    afr, afi, abr, abi = a8[0:1], a8[1:2], a8[2:3], a8[3:4]
    d = d_ref[...]

    nq = half // LANES

    def split(a):
        return [a[:, q * LANES:(q + 1) * LANES] for q in range(a.shape[1] // LANES)]

    def run(nb, m, init):
        pitch = m + SUBLANES
        for b in range(nb):
            for q in range(2 * nq):
                xf_s[q, b * pitch:b * pitch + m, :] = big[b * m:(b + 1) * m, kc + q * LANES:kc + (q + 1) * LANES]
                xb_s[q, b * pitch:b * pitch + m, :] = big[b * m:(b + 1) * m,
                                                          2 * kc + q * LANES:2 * kc + (q + 1) * LANES]

        def step(j, st, ar, ai, x_s, s_s):
            new_r, new_i = [], []
            for q in range(nq):
                sr, si = st[q], st[nq + q]
                s_s[q, pl.ds(j, nb, stride=pitch), :] = sr
                s_s[nq + q, pl.ds(j, nb, stride=pitch), :] = si
                xr = x_s[q, pl.ds(j, nb, stride=pitch), :]
                xi = x_s[nq + q, pl.ds(j, nb, stride=pitch), :]
                new_r.append(ar[q] * sr - ai[q] * si + xr)
                new_i.append(ar[q] * si + ai[q] * sr + xi)
            return tuple(new_r + new_i)

        fr, fi, br_, bi_ = split(afr), split(afi), split(abr), split(abi)
        fin_f = lax.fori_loop(0, m, lambda j, st: step(j, st, fr, fi, xf_s, sf_s),
                              tuple(split(init[0]) + split(init[1])))
        fin_b = lax.fori_loop(0, m, lambda jj, st: step(m - 1 - jj, st, br_, bi_, xb_s, sb_s),
                              tuple(split(init[2]) + split(init[3])))
        srows = jnp.concatenate(
            [jnp.concatenate([s_s[q, b * pitch:b * pitch + m, :] for s_s in (sf_s, sb_s) for q in range(2 * nq)],
                             axis=1) for b in range(nb)], axis=0).astype(BF16)
        y = big[:, :kc] + lax.dot_general(srows, qt_ref[...], (((1,), (1,)), ((), ())),
                                          preferred_element_type=F32)
        for t in range(CHUNK):
            zs_s[pl.ds(t, rows, stride=CHUNK), :] = jax.nn.gelu(y[:, t * LANES:(t + 1) * LANES] + d * us[t])
        z_ref[...] = zs_s[...].astype(BF16)
        return [jnp.concatenate(f[h * nq:(h + 1) * nq], axis=1) for f in (fin_f, fin_b) for h in range(2)]

    @pl.when(p < npp)
    def _():
        zero = jnp.zeros((nbp, half), F32)
        fin = run(nbp, mp, (zero, zero, zero, zero))
        for q in range(4):
            fin_ref[q] = fin[q]

    @pl.when(p >= npp)
    def _():
        run(nbs, ms, tuple(h0_ref[q] for q in range(4)))


def _s5_scan(u, d_skip, layer, wall, qt, a8, h0, tp, seq, ls):
    t, dm = u.shape
    nslab, half = a8.shape[0], a8.shape[-1]
    kc = CHUNK * LANES
    r = min(4096, tp)
    assert tp % r == 0 and (t - tp) % r == 0 and r % seq == 0 and r % ls == 0
    npp, nparts = tp // r, t // r
    nbp, mp, nbs, ms = r // seq, seq // CHUNK, r // ls, ls // CHUNK
    nsp = nparts - npp
    srows = max(nbp * (mp + SUBLANES), nbs * (ms + SUBLANES))
    z, fin = pl.pallas_call(
        functools.partial(_s5_kernel, r=r, npp=npp, nbp=nbp, mp=mp, nbs=nbs, ms=ms),
        out_shape=(jax.ShapeDtypeStruct((t, dm), BF16),
                   jax.ShapeDtypeStruct((npp, 4, nbp, nslab * half), F32)),
        grid=(nslab, nparts),
        in_specs=[pl.BlockSpec((r, LANES), lambda c, p: (p, c)),
                  pl.BlockSpec((None, 1, LANES), lambda c, p: (layer, 0, c)),
                  pl.BlockSpec((None, kc, 3 * kc), lambda c, p: (c, 0, 0)),
                  pl.BlockSpec((None, kc, 2 * kc), lambda c, p: (c, 0, 0)),
                  pl.BlockSpec((None, SUBLANES, half), lambda c, p: (c, 0, 0)),
                  pl.BlockSpec((None, 4, nbs, half), lambda c, p: (jnp.clip(p - npp, 0, nsp - 1), 0, 0, c))],
        out_specs=(pl.BlockSpec((r, LANES), lambda c, p: (p, c)),
                   pl.BlockSpec((None, 4, nbp, half), lambda c, p: (jnp.minimum(p, npp - 1), 0, 0, c))),
        scratch_shapes=[pltpu.VMEM((2 * half // LANES, srows, LANES), F32)] * 4 + [pltpu.VMEM((r, LANES), F32)],
        compiler_params=_params(("arbitrary", "arbitrary")),
        name="s5_scan",
    )(u, d_skip.reshape(d_skip.shape[0], 1, dm), wall, qt, a8, h0)
    return z, fin


def _gmlp_kernel(u_ref, v_ref, g_ref, b_ref, ws_ref, bs_ref, o_ref, *, heads, hd):
    v = _ln(v_ref[...].astype(F32), g_ref[...], b_ref[...]).astype(BF16)
    for h in range(heads):
        sl = slice(h * hd, (h + 1) * hd)
        s = jnp.dot(ws_ref[h].astype(BF16), v[:, sl], preferred_element_type=F32) + bs_ref[:, h:h + 1]
        o_ref[:, sl] = (u_ref[:, sl].astype(F32) * s).astype(BF16)


def _gmlp_gate(z, g_ln_g, g_ln_b, w_s, b_s_t, layer):
    t, d2 = z.shape
    d = d2 // 2
    _, heads, ck, _ = w_s.shape
    return pl.pallas_call(
        functools.partial(_gmlp_kernel, heads=heads, hd=d // heads),
        out_shape=jax.ShapeDtypeStruct((t, d), BF16),
        grid=(t // ck,),
        in_specs=[pl.BlockSpec((ck, d), lambda i: (i, 0)),
                  pl.BlockSpec((ck, d), lambda i: (i, 1)),
                  pl.BlockSpec((None, 1, d), lambda i: (layer, 0, 0)),
                  pl.BlockSpec((None, 1, d), lambda i: (layer, 0, 0)),
                  pl.BlockSpec((None, heads, ck, ck), lambda i: (layer, 0, 0, 0)),
                  pl.BlockSpec((None, ck, heads), lambda i: (layer, 0, 0))],
        out_specs=pl.BlockSpec((ck, d), lambda i: (i, 0)),
        compiler_params=_params(("arbitrary",)),
        name="gmlp_gate",
    )(z, z, g_ln_g.reshape(-1, 1, d), g_ln_b.reshape(-1, 1, d), w_s, b_s_t)


def _second_of4(a, b, c, d):
    return jnp.maximum(jnp.maximum(jnp.minimum(a, b), jnp.minimum(c, d)),
                       jnp.minimum(jnp.maximum(a, b), jnp.maximum(c, d)))


def _first_argmax(vals):
    idx = jnp.zeros(vals[0].shape, I32)
    best = vals[0]
    for k in range(1, len(vals)):
        upd = vals[k] > best
        idx = jnp.where(upd, k, idx)
        best = jnp.where(upd, vals[k], best)
    return idx


def _pick(idx, vals):
    out = vals[0]
    for k in range(1, len(vals)):
        out = jnp.where(idx == k, vals[k], out)
    return out


def _route(h, wr, br):
    n_exp = wr.shape[0]
    epg = n_exp // N_EXPERT_GROUPS
    logits = lax.dot_general(wr, h, (((1,), (1,)), ((), ())), precision=HIGHEST, preferred_element_type=F32)
    ex = jnp.exp(logits - jnp.max(logits, axis=0, keepdims=True))
    probs = ex / jnp.sum(ex, axis=0, keepdims=True)
    sel = probs + br
    srow = [sel[e:e + 1] for e in range(n_exp)]
    prow = [probs[e:e + 1] for e in range(n_exp)]
    scores = []
    for gi in range(N_EXPERT_GROUPS):
        a, b, c, d = srow[gi * epg:(gi + 1) * epg]
        scores.append(jnp.maximum(jnp.maximum(a, b), jnp.maximum(c, d)) + _second_of4(a, b, c, d))
    bg = _first_argmax(scores)
    vs = [_pick(bg, [srow[gi * epg + k] for gi in range(N_EXPERT_GROUPS)]) for k in range(epg)]
    ps = [_pick(bg, [prow[gi * epg + k] for gi in range(N_EXPERT_GROUPS)]) for k in range(epg)]
    i1 = _first_argmax(vs)
    i2 = _first_argmax([jnp.where(i1 == k, -jnp.inf, vs[k]) for k in range(epg)])
    g1, g2 = _pick(i1, ps), _pick(i2, ps)
    tot = g1 + g2
    zero = jnp.zeros_like(g1)
    return jnp.concatenate([(bg * epg + i1).astype(F32), (bg * epg + i2).astype(F32), g1 / tot, g2 / tot,
                            zero, zero, zero, zero], axis=0)


def _postnorm_a_kernel(x_ref, y_ref, gate_ref, g_ref, b_ref, sh_ref, sc_ref, wr_ref, br_ref,
                       x1_ref, hp_ref, route_ref, *, tm, tp, ls, alpha, pitch):
    r = _cond_row(pl.program_id(0), tm, tp, ls)
    v = alpha * x_ref[...] + gate_ref[pl.ds(r, 1), :] * y_ref[...].astype(F32)
    x1 = _ln(v, g_ref[...], b_ref[...])
    x1_ref[...] = x1
    h = x1 * (1 + sc_ref[pl.ds(r, 1), :]) + sh_ref[pl.ds(r, 1), :]
    _store_packed(hp_ref, h, tm, pitch)
    route_ref[...] = _route(h, wr_ref[...], br_ref[...])


def _postnorm_a(x, y, mods, layer, ln_g, ln_b, wr, br, tm, tp, ls, alpha, pitch):
    t, d = x.shape
    n_exp = wr.shape[0]
    row = lambda i: (i, 0)
    lnspec = pl.BlockSpec((None, None, 1, d), lambda i: (layer, 0, 0, 0))
    return pl.pallas_call(
        functools.partial(_postnorm_a_kernel, tm=tm, tp=tp, ls=ls, alpha=alpha, pitch=pitch),
        out_shape=(jax.ShapeDtypeStruct((t, d), F32), jax.ShapeDtypeStruct((t * pitch, LANES), U32),
                   jax.ShapeDtypeStruct((SUBLANES, t), F32)),
        grid=(t // tm,),
        in_specs=[pl.BlockSpec((tm, d), row), pl.BlockSpec((tm, d), row), _mod_spec(d, layer, 2),
                  lnspec, lnspec, _mod_spec(d, layer, 3), _mod_spec(d, layer, 4),
                  pl.BlockSpec((n_exp, d), lambda i: (0, 0)), pl.BlockSpec((n_exp, 1), lambda i: (0, 0))],
        out_specs=(pl.BlockSpec((tm, d), row), pl.BlockSpec((tm * pitch, LANES), row),
                   pl.BlockSpec((SUBLANES, tm), lambda i: (0, i))),
        compiler_params=_params(("arbitrary",)),
        name="postnorm_a",
    )(x, y, mods, ln_g.reshape(ln_g.shape[0], 2, 1, d), ln_b.reshape(ln_b.shape[0], 2, 1, d), mods, mods, wr, br)


def _postnorm_b_kernel(*refs, tm, tp, ls, alpha, pitch, nch, last):
    x_ref, y2_ref, cw_ref, gate_ref, g_ref, b_ref = refs[:6]
    r = _cond_row(pl.program_id(0), tm, tp, ls)
    c0, c1 = cw_ref[:, 0:1], cw_ref[:, 1:2]
    pieces = []
    for s in range(nch):
        lo0, hi0 = _unpack_pair(y2_ref[pl.ds(s, tm, stride=2 * pitch), :])
        lo1, hi1 = _unpack_pair(y2_ref[pl.ds(pitch + s, tm, stride=2 * pitch), :])
        pieces += [c0 * lo0 + c1 * lo1, c0 * hi0 + c1 * hi1]
    f = jnp.concatenate(pieces, axis=1)
    x2 = _ln(alpha * x_ref[...] + gate_ref[pl.ds(r, 1), :] * f, g_ref[...], b_ref[...])
    if last:
        refs[6][...] = x2
    else:
        sh_ref, sc_ref, x2_ref, h_ref = refs[6:]
        x2_ref[...] = x2
        h_ref[...] = (x2 * (1 + sc_ref[pl.ds(r, 1), :]) + sh_ref[pl.ds(r, 1), :]).astype(BF16)


def _postnorm_b(x, y2p, cw, mods, layer, ln_g, ln_b, tm, tp, ls, alpha, pitch, last):
    t, d = x.shape
    row = lambda i: (i, 0)
    lnspec = pl.BlockSpec((None, None, 1, d), lambda i: (layer, 1, 0, 0))
    in_specs = [pl.BlockSpec((tm, d), row), pl.BlockSpec((tm * 2 * pitch, LANES), row),
                pl.BlockSpec((tm, 2), row), _mod_spec(d, layer, 5), lnspec, lnspec]
    args = [x, y2p, cw, mods, ln_g.reshape(ln_g.shape[0], 2, 1, d), ln_b.reshape(ln_b.shape[0], 2, 1, d)]
    out_shape = [jax.ShapeDtypeStruct((t, d), F32)]
    out_specs = [pl.BlockSpec((tm, d), row)]
    if not last:
        in_specs += [_mod_spec(d, layer + 1, 0), _mod_spec(d, layer + 1, 1)]
        args += [mods, mods]
        out_shape.append(jax.ShapeDtypeStruct((t, d), BF16))
        out_specs.append(pl.BlockSpec((tm, d), row))
    out = pl.pallas_call(
        functools.partial(_postnorm_b_kernel, tm=tm, tp=tp, ls=ls, alpha=alpha, pitch=pitch,
                          nch=d // PAIR, last=last),
        out_shape=tuple(out_shape), grid=(t // tm,), in_specs=in_specs, out_specs=tuple(out_specs),
        compiler_params=_params(("arbitrary",)),
        name="postnorm_b",
    )(*args)
    return (out[0], None) if last else out


def _plan(route, tme, n_exp):
    t = route.shape[1]
    ef = route[0:2].astype(I32).T.reshape(-1)
    oh = (ef[:, None] == jnp.arange(n_exp, dtype=I32)[None]).astype(I32)
    cs = jnp.cumsum(oh, axis=0)
    counts = cs[-1]
    ntile = (counts + tme - 1) // tme
    tend = jnp.cumsum(ntile)
    tstart = tend - ntile
    pos = jnp.sum(oh * (tstart[None] * tme + cs - 1), axis=1)
    nt = (2 * t) // tme + n_exp
    tid = jnp.arange(nt, dtype=I32)
    texp = jnp.minimum(jnp.sum((tid[:, None] >= tend[None]).astype(I32), axis=1), n_exp - 1)
    first = jnp.concatenate([jnp.ones((1,), I32), (texp[1:] != texp[:-1]).astype(I32)])
    return dict(pos=pos, counts=counts, tstart=tstart, ntile=ntile, texp=texp, first=first,
                nvalid=tend[-1:], nt=nt, cw=route[2:4].T)


def _scatter_kernel(cnt_ref, tst_ref, ntl_ref, nv_ref, pos_ref, src_ref, zero_ref, dst_ref, sem, tsem,
                    *, rows, tme, n_exp, nt):
    i = pl.program_id(0)

    def wait_one(_, c):
        pltpu.make_async_copy(zero_ref.at[0], dst_ref.at[0], sem).wait()
        return c

    @pl.when(i == 0)
    def _():
        def tail(tl, c):
            pltpu.make_async_copy(zero_ref, dst_ref.at[pl.ds(tl * tme, tme)], tsem).start()
            return c

        def tail_wait(_, c):
            pltpu.make_async_copy(zero_ref, dst_ref.at[pl.ds(0, tme)], tsem).wait()
            return c

        lax.fori_loop(nv_ref[0], nt, tail, 0)
        lax.fori_loop(nv_ref[0], nt, tail_wait, 0)

        for e in range(n_exp):
            lo = tst_ref[e] * tme + cnt_ref[e]
            hi = (tst_ref[e] + ntl_ref[e]) * tme

            def fill(rw, c):
                pltpu.make_async_copy(zero_ref.at[0], dst_ref.at[rw], sem).start()
                return c

            lax.fori_loop(lo, hi, fill, 0)
            lax.fori_loop(lo, hi, wait_one, 0)

    def move(rw, c):
        a = i * rows + rw
        pltpu.make_async_copy(src_ref.at[lax.shift_right_logical(a, 1)], dst_ref.at[pos_ref[0, rw]], sem).start()
        return c

    lax.fori_loop(0, rows, move, 0)
    lax.fori_loop(0, rows, wait_one, 0)


def _scatter_rows(hp3, plan, tme, n_exp, rows=512):
    t, pitch, _ = hp3.shape
    na = 2 * t
    rows = min(rows, na)
    pos3 = plan["pos"].reshape(na // rows, 1, rows)
    return pl.pallas_call(
        functools.partial(_scatter_kernel, rows=rows, tme=tme, n_exp=n_exp, nt=plan["nt"]),
        out_shape=jax.ShapeDtypeStruct((plan["nt"] * tme, pitch, LANES), U32),
        grid_spec=pltpu.PrefetchScalarGridSpec(
            num_scalar_prefetch=4, grid=(na // rows,),
            in_specs=[pl.BlockSpec((None, 1, rows), lambda i, *_: (i, 0, 0), memory_space=pltpu.SMEM),
                      pl.BlockSpec(memory_space=pl.ANY), pl.BlockSpec(memory_space=pl.ANY)],
            out_specs=pl.BlockSpec(memory_space=pl.ANY),
            scratch_shapes=[pltpu.SemaphoreType.DMA(()), pltpu.SemaphoreType.DMA(())]),
        compiler_params=pltpu.CompilerParams(dimension_semantics=("arbitrary",), has_side_effects=True),
        name="moe_scatter",
    )(plan["counts"], plan["tstart"], plan["ntile"], plan["nvalid"], pos3, hp3,
      jnp.zeros((tme, pitch, LANES), U32))


def _gather_kernel(pos_ref, src_ref, dst_ref, sem, *, rows):
    i = pl.program_id(0)

    def move(rw, c):
        pltpu.make_async_copy(src_ref.at[pos_ref[0, rw]], dst_ref.at[i * rows + rw], sem).start()
        return c

    def wait_one(_, c):
        pltpu.make_async_copy(src_ref.at[0], dst_ref.at[0], sem).wait()
        return c

    lax.fori_loop(0, rows, move, 0)
    lax.fori_loop(0, rows, wait_one, 0)


def _gather_rows(src3, pos, rows=512):
    na = pos.shape[0]
    rows = min(rows, na)
    _, pitch, _ = src3.shape
    return pl.pallas_call(
        functools.partial(_gather_kernel, rows=rows),
        out_shape=jax.ShapeDtypeStruct((na, pitch, LANES), U32),
        grid=(na // rows,),
        in_specs=[pl.BlockSpec((None, 1, rows), lambda i: (i, 0, 0), memory_space=pltpu.SMEM),
                  pl.BlockSpec(memory_space=pl.ANY)],
        out_specs=pl.BlockSpec(memory_space=pl.ANY),
        scratch_shapes=[pltpu.SemaphoreType.DMA(())],
        compiler_params=pltpu.CompilerParams(dimension_semantics=("arbitrary",), has_side_effects=True),
        name="moe_gather",
    )(pos.reshape(na // rows, 1, rows), src3)


def _moe_a_kernel(te_ref, first_ref, nv_ref, xs_ref, wg_ref, wu_ref, h_ref, lhs_s, wg_s, wu_s,
                  *, tme, nch, pitch):
    i = pl.program_id(1)

    @pl.when(i < nv_ref[0])
    def _():
        @pl.when(first_ref[i] == 1)
        def _():
            wg_s[...] = wg_ref[...].astype(BF16)
            wu_s[...] = wu_ref[...].astype(BF16)

        for s in range(nch):
            lo, hi = _unpack_pair(xs_ref[pl.ds(s, tme, stride=pitch), :])
            lhs_s[:, s * PAIR:s * PAIR + LANES] = lo.astype(BF16)
            lhs_s[:, s * PAIR + LANES:(s + 1) * PAIR] = hi.astype(BF16)
        x = lhs_s[...]
        g = jnp.dot(x, wg_s[...], preferred_element_type=F32)
        u = jnp.dot(x, wu_s[...], preferred_element_type=F32)
        h_ref[...] = (jax.nn.silu(g) * u).astype(BF16)

    @pl.when(i >= nv_ref[0])
    def _():
        h_ref[...] = jnp.zeros_like(h_ref)


def _moe_a(xs2, plan, w_gate, w_up, layer, tme, pitch):
    _, _, d, f = w_gate.shape
    nt = plan["nt"]
    tna = min(512, f)

    def tile(i, nv):
        return jnp.minimum(i, nv[0] - 1)

    wspec = pl.BlockSpec((None, None, d, tna), lambda fi, i, te, fr, nv: (layer, te[tile(i, nv)], 0, fi),
                         pipeline_mode=pl.Buffered(1))
    return pl.pallas_call(
        functools.partial(_moe_a_kernel, tme=tme, nch=d // PAIR, pitch=pitch),
        out_shape=jax.ShapeDtypeStruct((nt * tme, f), BF16),
        grid_spec=pltpu.PrefetchScalarGridSpec(
            num_scalar_prefetch=3, grid=(f // tna, nt),
            in_specs=[pl.BlockSpec((tme * pitch, LANES), lambda fi, i, te, fr, nv: (tile(i, nv), 0)), wspec, wspec],
            out_specs=pl.BlockSpec((tme, tna), lambda fi, i, te, fr, nv: (i, fi)),
            scratch_shapes=[pltpu.VMEM((tme, d), BF16), pltpu.VMEM((d, tna), BF16), pltpu.VMEM((d, tna), BF16)]),
        compiler_params=_params(("arbitrary", "arbitrary")),
        name="moe_up",
    )(plan["texp"], plan["first"], plan["nvalid"], xs2, w_gate, w_up)


def _moe_b_kernel(te_ref, first_ref, nv_ref, h_ref, wd_ref, ys_ref, wd_s, *, tme, pitch):
    i = pl.program_id(0)

    @pl.when(i < nv_ref[0])
    def _():
        @pl.when(first_ref[i] == 1)
        def _():
            wd_s[...] = wd_ref[...].astype(BF16)

        _store_packed(ys_ref, jnp.dot(h_ref[...], wd_s[...], preferred_element_type=F32), tme, pitch)

    @pl.when(i >= nv_ref[0])
    def _():
        ys_ref[...] = jnp.zeros_like(ys_ref)


def _moe_b(hmid, plan, w_down, layer, tme, pitch):
    _, _, f, d = w_down.shape
    nt = plan["nt"]

    def tile(i, nv):
        return jnp.minimum(i, nv[0] - 1)

    return pl.pallas_call(
        functools.partial(_moe_b_kernel, tme=tme, pitch=pitch),
        out_shape=jax.ShapeDtypeStruct((nt * tme * pitch, LANES), U32),
        grid_spec=pltpu.PrefetchScalarGridSpec(
            num_scalar_prefetch=3, grid=(nt,),
            in_specs=[pl.BlockSpec((tme, f), lambda i, te, fr, nv: (tile(i, nv), 0)),
                      pl.BlockSpec((None, None, f, d), lambda i, te, fr, nv: (layer, te[tile(i, nv)], 0, 0),
                                   pipeline_mode=pl.Buffered(1))],
            out_specs=pl.BlockSpec((tme * pitch, LANES), lambda i, te, fr, nv: (i, 0)),
            scratch_shapes=[pltpu.VMEM((f, d), BF16)]),
        compiler_params=_params(("arbitrary",)),
        name="moe_down",
    )(plan["texp"], plan["first"], plan["nvalid"], hmid, w_down)


def _pos2d(length, d):
    rows = length // GRID_W
    r = jnp.repeat(jnp.arange(rows, dtype=F32), GRID_W)
    col = jnp.tile(jnp.arange(GRID_W, dtype=F32), rows)
    nf = d // 4
    omega = 1.0 / (10000.0 ** (jnp.arange(nf, dtype=F32) / nf))

    def enc(p):
        a = p[:, None] * omega[None, :]
        return jnp.concatenate([jnp.sin(a), jnp.cos(a)], axis=-1)

    return jnp.concatenate([enc(r), enc(col)], axis=-1)


def kernel(x_prompt, x_sample, state_ssm_re, state_ssm_im, c, c_ctx, w_ada, b_ada, ln_g, ln_b, s5_w_in, s5_lam_re, s5_lam_im, s5_log_dt, s5_b_re, s5_b_im, s5_c_re, s5_c_im, s5_d, s5_w_glu, s5_w_out, g_w_in, g_b_in, g_ln_g, g_ln_b, g_w_s, g_b_s, g_w_out, w_router, b_router, moe_w_gate, moe_w_up, moe_w_down):
    bp, seq, d = x_prompt.shape
    bs, ls, _ = x_sample.shape
    depth = w_ada.shape[0]
    n_exp = w_router.shape[1]
    tp, ts = bp * seq, bs * ls
    alpha = float((2 * depth) ** 0.25)
    tm = min(256, seq)
    tme = min(512, tp)
    pitch = _pitch(d // PAIR)
    assert bs + 1 <= COND_ROWS and d % PAIR == 0 and seq % tm == 0 and ls % tm == 0

    cond = jnp.concatenate([c_ctx[None].astype(F32), c.astype(F32),
                            jnp.zeros((COND_ROWS - 1 - bs, d), F32)], axis=0)
    mods = _adaln(cond, w_ada, b_ada)
    x, h = _embed(x_prompt.reshape(tp, d), x_sample.reshape(ts, d), _pos2d(ls, d), mods, tm, ls)
    wr, br = w_router.T, b_router.reshape(n_exp, 1)
    g_b_s_t = jnp.swapaxes(g_b_s, -1, -2)

    new_re, new_im = [], []
    for i in range(depth):
        j = i // 2
        if i % 2 == 0:
            wall, qt, a8 = _s5_prep(s5_lam_re[j], s5_lam_im[j], s5_log_dt[j], s5_b_re[j], s5_b_im[j],
                                    s5_c_re[j], s5_c_im[j])
            g, n = s5_lam_re.shape[2:]
            nsp = ts // min(4096, tp)
            h0 = jnp.stack([state_ssm_re[:, j, 0], state_ssm_im[:, j, 0], state_ssm_re[:, j, 1],
                            state_ssm_im[:, j, 1]]).reshape(4, nsp, bs // nsp, g * n).swapaxes(0, 1)
            u, _ = _matmul(h, s5_w_in, (j,), name="s5_in")
            z, fin = _s5_scan(u, s5_d, j, wall, qt, a8, h0, tp, seq, ls)
            fin = fin.swapaxes(0, 1).reshape(4, bp, g, n)
            new_re.append(jnp.stack([fin[0], fin[2]], axis=1))
            new_im.append(jnp.stack([fin[1], fin[3]], axis=1))
            tmm, tnn = min(1024, tp + ts), min(1024, d)
            zg, _ = _matmul(z, s5_w_glu, (j,), epilogue=_epi_glu, extra=(z,),
                            extra_specs=(pl.BlockSpec((tmm, tnn), lambda a, b, k: (a, b)),),
                            out_dtype=BF16, name="s5_glu")
            ymix, _ = _matmul(zg, s5_w_out, (j,), out_dtype=BF16, name="s5_out")
        else:
            tnn = min(1024, 2 * d)
            zz, _ = _matmul(h, g_w_in, (j,), epilogue=_epi_bias_gelu,
                            extra=(g_b_in.reshape(-1, 1, 2 * d),),
                            extra_specs=(pl.BlockSpec((None, 1, tnn), lambda a, b, k: (j, 0, b)),),
                            out_dtype=BF16, name="gmlp_in")
            gated = _gmlp_gate(zz, g_ln_g, g_ln_b, g_w_s, g_b_s_t, j)
            ymix, _ = _matmul(gated, g_w_out, (j,), out_dtype=BF16, name="gmlp_out")
        x, hp, route = _postnorm_a(x, ymix, mods, i, ln_g, ln_b, wr, br, tm, tp, ls, alpha, pitch)
        plan = _plan(route, tme, n_exp)
        xs = _scatter_rows(hp.reshape(tp + ts, pitch, LANES), plan, tme, n_exp)
        hmid = _moe_a(xs.reshape(-1, LANES), plan, moe_w_gate, moe_w_up, i, tme, pitch)
        ysp = _moe_b(hmid, plan, moe_w_down, i, tme, pitch)
        y2p = _gather_rows(ysp.reshape(-1, pitch, LANES), plan["pos"])
        x, h = _postnorm_b(x, y2p.reshape(-1, LANES), plan["cw"], mods, i, ln_g, ln_b, tm, tp, ls, alpha,
                           pitch, last=(i == depth - 1))
    return (x[:tp].reshape(bp, seq, d), x[tp:].reshape(bs, ls, d),
            jnp.stack(new_re, axis=1).astype(x_prompt.dtype), jnp.stack(new_im, axis=1).astype(x_prompt.dtype))
```

```python
import functools

import jax
import jax.numpy as jnp
from jax import lax
from jax.experimental import pallas as pl
from jax.experimental.pallas import tpu as pltpu

F32, BF16, U32, I32 = jnp.float32, jnp.bfloat16, jnp.uint32, jnp.int32

LN_EPS = 1e-5
GRID_W = 64
N_MOD = 6
N_EXPERT_GROUPS = 4
COND_ROWS = 16
LANES = 128
SUBLANES = 8
CHUNK = SUBLANES
PAIR = 2 * LANES
V7X_VMEM_LIMIT = 56 * 2**20
HIGHEST = lax.Precision.HIGHEST


def _params(sem, vmem=V7X_VMEM_LIMIT):
    return pltpu.CompilerParams(dimension_semantics=sem, vmem_limit_bytes=vmem)


def _cond_row(tile, tm, tp, ls):
    start = tile * tm
    return jnp.where(start < tp, 0, 1 + jnp.maximum(start - tp, 0) // ls)


def _ln(x, g, b):
    mu = jnp.mean(x, axis=-1, keepdims=True)
    xc = x - mu
    var = jnp.mean(xc * xc, axis=-1, keepdims=True)
    return xc * lax.rsqrt(var + LN_EPS) * g + b


def _pitch(nchunks):
    p = -(-nchunks // SUBLANES) * SUBLANES
    if (p // SUBLANES) % 2 == 0:
        p += SUBLANES
    return p


def _pack_pair(a, b):
    ua = lax.bitcast_convert_type(a.astype(BF16).astype(F32), U32)
    ub = lax.bitcast_convert_type(b.astype(BF16).astype(F32), U32)
    return (ua >> 16) | ub


def _unpack_pair(w):
    lo = lax.bitcast_convert_type(w << 16, F32)
    hi = lax.bitcast_convert_type(w & jnp.uint32(0xFFFF0000), F32)
    return lo, hi


def _store_packed(ref, h, rows, pitch):
    nch = h.shape[1] // PAIR
    for s in range(nch):
        ref[pl.ds(s, rows, stride=pitch), :] = _pack_pair(
            h[:, s * PAIR:s * PAIR + LANES], h[:, s * PAIR + LANES:(s + 1) * PAIR])
    for s in range(nch, pitch):
        ref[pl.ds(s, rows, stride=pitch), :] = jnp.zeros((rows, LANES), U32)


def _adaln_kernel(c_ref, w_ref, b_ref, o_ref):
    s = jax.nn.silu(c_ref[...]).astype(BF16)
    o_ref[...] = jnp.dot(s, w_ref[...].astype(BF16), preferred_element_type=F32) + b_ref[...]


def _adaln(cond, w_ada, b_ada):
    depth, d, nm = w_ada.shape
    tn = min(512, nm)
    return pl.pallas_call(
        _adaln_kernel,
        out_shape=jax.ShapeDtypeStruct((depth, COND_ROWS, nm), F32),
        grid=(depth, nm // tn),
        in_specs=[pl.BlockSpec((COND_ROWS, d), lambda i, n: (0, 0)),
                  pl.BlockSpec((None, d, tn), lambda i, n: (i, 0, n)),
                  pl.BlockSpec((None, 1, tn), lambda i, n: (i, 0, n))],
        out_specs=pl.BlockSpec((None, COND_ROWS, tn), lambda i, n: (i, 0, n)),
        compiler_params=_params(("arbitrary", "arbitrary")),
        name="adaln",
    )(cond, w_ada, b_ada.reshape(depth, 1, nm))


def _mod_spec(d, layer, m):
    return pl.BlockSpec((None, COND_ROWS, d), lambda *_: (layer, 0, m))


def _embed_kernel(xp_ref, xs_ref, pos_ref, sh_ref, sc_ref, x_ref, h_ref, *, tm, tp, ls):
    i = pl.program_id(0)
    r = _cond_row(i, tm, tp, ls)
    sh = sh_ref[pl.ds(r, 1), :]
    sc = sc_ref[pl.ds(r, 1), :]

    def emit(x):
        x_ref[...] = x
        h_ref[...] = (x * (1 + sc) + sh).astype(BF16)

    @pl.when(i * tm < tp)
    def _():
        emit(xp_ref[...])

    @pl.when(i * tm >= tp)
    def _():
        emit(xs_ref[...] + pos_ref[...])


def _embed(xp, xs, pos, mods, tm, ls):
    tp, d = xp.shape
    ts = xs.shape[0]
    npt, t = tp // tm, tp + ts
    nps = ls // tm
    return pl.pallas_call(
        functools.partial(_embed_kernel, tm=tm, tp=tp, ls=ls),
        out_shape=(jax.ShapeDtypeStruct((t, d), F32), jax.ShapeDtypeStruct((t, d), BF16)),
        grid=(t // tm,),
        in_specs=[pl.BlockSpec((tm, d), lambda i: (jnp.minimum(i, npt - 1), 0)),
                  pl.BlockSpec((tm, d), lambda i: (jnp.maximum(i - npt, 0), 0)),
                  pl.BlockSpec((tm, d), lambda i: (jnp.maximum(i - npt, 0) % nps, 0)),
                  _mod_spec(d, 0, 0), _mod_spec(d, 0, 1)],
        out_specs=(pl.BlockSpec((tm, d), lambda i: (i, 0)), pl.BlockSpec((tm, d), lambda i: (i, 0))),
        compiler_params=_params(("arbitrary",)),
        name="embed",
    )(xp, xs, pos, mods, mods)


def _epi_store(acc, extra, outs):
    outs[0][...] = acc.astype(outs[0].dtype)


def _epi_bias_gelu(acc, extra, outs):
    outs[0][...] = jax.nn.gelu(acc + extra[0][...]).astype(outs[0].dtype)


def _epi_glu(acc, extra, outs):
    outs[0][...] = (extra[0][...].astype(F32) * jax.nn.sigmoid(acc)).astype(outs[0].dtype)


def _mm_kernel(*refs, nk, n_extra, epilogue):
    x_ref, w_ref = refs[:2]
    extra = refs[2:2 + n_extra]
    part = jnp.dot(x_ref[...].astype(BF16), w_ref[...].astype(BF16), preferred_element_type=F32)
    if nk == 1:
        epilogue(part, extra, refs[2 + n_extra:])
        return
    outs, acc = refs[2 + n_extra:-1], refs[-1]
    k = pl.program_id(2)

    @pl.when(k == 0)
    def _():
        acc[...] = part

    @pl.when(k > 0)
    def _():
        acc[...] += part

    @pl.when(k == nk - 1)
    def _():
        epilogue(acc[...], extra, outs)


def _matmul(x, w, w_prefix, *, epilogue=_epi_store, extra=(), extra_specs=(), out_dtype=F32, name):
    m, kdim = x.shape
    n = w.shape[-1]
    tm, tn, tk = min(1024, m), min(1024, n), min(2048, kdim)
    nk = kdim // tk
    npre = len(w_prefix)
    return pl.pallas_call(
        functools.partial(_mm_kernel, nk=nk, n_extra=len(extra), epilogue=epilogue),
        out_shape=jax.ShapeDtypeStruct((m, n), out_dtype),
        grid=(m // tm, n // tn, nk),
        in_specs=[pl.BlockSpec((tm, tk), lambda i, j, k: (i, k)),
                  pl.BlockSpec((None,) * npre + (tk, tn), lambda i, j, k: tuple(w_prefix) + (k, j)),
                  *extra_specs],
        out_specs=pl.BlockSpec((tm, tn), lambda i, j, k: (i, j)),
        scratch_shapes=[pltpu.VMEM((tm, tn), F32)] if nk > 1 else [],
        compiler_params=_params(("arbitrary", "arbitrary", "arbitrary")),
        name=name,
    )(x, w, *extra), (tm, tn)


def _s5_prep_kernel(lre_ref, lim_ref, ldt_ref, btr_ref, bti_ref, ctr_ref, cti_ref,
                    wall_ref, qt_ref, a8_ref, *, group_ch, state_n):
    half = lre_ref.shape[-1]
    kc = CHUNK * LANES
    mask = (lax.broadcasted_iota(I32, (LANES, half), 0) // group_ch
            == lax.broadcasted_iota(I32, (LANES, half), 1) // state_n)
    tstacks, a8rows = [], []
    for k in range(2):
        lr, li = lre_ref[k], lim_ref[k]
        dt = jnp.exp(ldt_ref[k])
        mag = jnp.exp(lr * dt)
        ar, ai = mag * jnp.cos(li * dt), mag * jnp.sin(li * dt)
        den = lr * lr + li * li
        cr = ((ar - 1) * lr + ai * li) / den
        ci = (ai * lr - (ar - 1) * li) / den
        pw = [(jnp.ones_like(ar), jnp.zeros_like(ar))]
        for _ in range(CHUNK):
            pr, pi = pw[-1]
            pw.append((pr * ar - pi * ai, pr * ai + pi * ar))
        br, bi = btr_ref[k], bti_ref[k]
        crr, cii = ctr_ref[k], cti_ref[k]
        ptiles = []
        for s in range(CHUNK):
            e = CHUNK - 1 - s if k == 0 else s
            zr = cr * pw[e][0] - ci * pw[e][1]
            zi = cr * pw[e][1] + ci * pw[e][0]
            tile = jnp.concatenate([jnp.where(mask, zr * br - zi * bi, 0.0),
                                    jnp.where(mask, zr * bi + zi * br, 0.0)], axis=1)
            ptiles.append(tile)
            wall_ref[s * LANES:(s + 1) * LANES, (1 + k) * kc:(2 + k) * kc] = tile.astype(BF16)
        qt0 = jnp.concatenate([jnp.where(mask, crr, 0.0), jnp.where(mask, -cii, 0.0)], axis=1)
        tstacks.append(lax.dot_general(jnp.concatenate(ptiles, axis=0), qt0, (((1,), (1,)), ((), ())),
                                       precision=HIGHEST, preferred_element_type=F32))
        for t in range(CHUNK):
            e = t + 1 if k == 0 else CHUNK - t
            tile = jnp.concatenate([jnp.where(mask, pw[e][0] * crr - pw[e][1] * cii, 0.0),
                                    jnp.where(mask, -(pw[e][0] * cii + pw[e][1] * crr), 0.0)], axis=1)
            qt_ref[t * LANES:(t + 1) * LANES, k * kc:(k + 1) * kc] = tile.astype(BF16)
        a8rows += [pw[CHUNK][0], pw[CHUNK][1]]
    tf, tb = tstacks
    for s in range(CHUNK):
        for t in range(CHUNK):
            if t > s:
                blk = CHUNK - 1 - (t - s)
                tile = tf[blk * LANES:(blk + 1) * LANES]
            elif s > t:
                tile = tb[(s - t) * LANES:(s - t + 1) * LANES]
            else:
                tile = tf[(CHUNK - 1) * LANES:] + tb[:LANES]
            wall_ref[s * LANES:(s + 1) * LANES, t * LANES:(t + 1) * LANES] = tile.astype(BF16)
    a8_ref[...] = jnp.concatenate(a8rows + [jnp.zeros_like(a8rows[0])] * (SUBLANES - 4), axis=0)


def _s5_prep(lam_re, lam_im, log_dt, b_re, b_im, c_re, c_im):
    _, g, n = lam_re.shape
    gc = b_re.shape[-1]
    gps = LANES // gc
    nslab, half = g // gps, gps * n
    kc = CHUNK * LANES

    def lane(a):
        return a.reshape(2, nslab, 1, half)

    def rows(a):
        return jnp.tile(a.reshape(2, nslab, LANES, n), (1, 1, 1, gps))

    ins = (lane(lam_re), lane(lam_im), lane(jnp.broadcast_to(log_dt[..., None], lam_re.shape)),
           rows(jnp.swapaxes(b_re, -1, -2)), rows(jnp.swapaxes(b_im, -1, -2)), rows(c_re), rows(c_im))
    lspec = pl.BlockSpec((2, None, 1, half), lambda c: (0, c, 0, 0))
    rspec = pl.BlockSpec((2, None, LANES, half), lambda c: (0, c, 0, 0))
    return pl.pallas_call(
        functools.partial(_s5_prep_kernel, group_ch=gc, state_n=n),
        out_shape=(jax.ShapeDtypeStruct((nslab, kc, 3 * kc), BF16),
                   jax.ShapeDtypeStruct((nslab, kc, 2 * kc), BF16),
                   jax.ShapeDtypeStruct((nslab, SUBLANES, half), F32)),
        grid=(nslab,),
        in_specs=[lspec, lspec, lspec, rspec, rspec, rspec, rspec],
        out_specs=(pl.BlockSpec((None, kc, 3 * kc), lambda c: (c, 0, 0)),
                   pl.BlockSpec((None, kc, 2 * kc), lambda c: (c, 0, 0)),
                   pl.BlockSpec((None, SUBLANES, half), lambda c: (c, 0, 0))),
        compiler_params=_params(("arbitrary",)),
        name="s5_prep",
    )(*ins)


def _s5_kernel(u_ref, d_ref, wall_ref, qt_ref, a8_ref, h0_ref, z_ref, fin_ref,
               xf_s, xb_s, sf_s, sb_s, zs_s, *, r, npp, nbp, mp, nbs, ms):
    p = pl.program_id(1)
    rows = r // CHUNK
    kc = CHUNK * LANES
    half = a8_ref.shape[-1]
    us = [u_ref[pl.ds(s, rows, stride=CHUNK), :] for s in range(CHUNK)]
    lhs = jnp.concatenate([x.astype(BF16) for x in us], axis=1)
    big = jnp.dot(lhs, wall_ref[...], preferred_element_type=F32)
    a8 = a8_ref[...]
    afr, afi, abr, abi = a8[0:1], a8[1:2], a8[2:3], a8[3:4]
    d = d_ref[...]

    nq = half // LANES

    def split(a):
        return [a[:, q * LANES:(q + 1) * LANES] for q in range(a.shape[1] // LANES)]

    def run(nb, m, init):
        pitch = m + SUBLANES
        for b in range(nb):
            for q in range(2 * nq):
                xf_s[q, b * pitch:b * pitch + m, :] = big[b * m:(b + 1) * m, kc + q * LANES:kc + (q + 1) * LANES]
                xb_s[q, b * pitch:b * pitch + m, :] = big[b * m:(b + 1) * m,
                                                          2 * kc + q * LANES:2 * kc + (q + 1) * LANES]

        def step(j, st, ar, ai, x_s, s_s):
            new_r, new_i = [], []
            for q in range(nq):
                sr, si = st[q], st[nq + q]
                s_s[q, pl.ds(j, nb, stride=pitch), :] = sr
                s_s[nq + q, pl.ds(j, nb, stride=pitch), :] = si
                xr = x_s[q, pl.ds(j, nb, stride=pitch), :]
                xi = x_s[nq + q, pl.ds(j, nb, stride=pitch), :]
                new_r.append(ar[q] * sr - ai[q] * si + xr)
                new_i.append(ar[q] * si + ai[q] * sr + xi)
            return tuple(new_r + new_i)

        fr, fi, br_, bi_ = split(afr), split(afi), split(abr), split(abi)
        fin_f = lax.fori_loop(0, m, lambda j, st: step(j, st, fr, fi, xf_s, sf_s),
                              tuple(split(init[0]) + split(init[1])))
        fin_b = lax.fori_loop(0, m, lambda jj, st: step(m - 1 - jj, st, br_, bi_, xb_s, sb_s),
                              tuple(split(init[2]) + split(init[3])))
        srows = jnp.concatenate(
            [jnp.concatenate([s_s[q, b * pitch:b * pitch + m, :] for s_s in (sf_s, sb_s) for q in range(2 * nq)],
                             axis=1) for b in range(nb)], axis=0).astype(BF16)
        y = big[:, :kc] + lax.dot_general(srows, qt_ref[...], (((1,), (1,)), ((), ())),
                                          preferred_element_type=F32)
        for t in range(CHUNK):
            zs_s[pl.ds(t, rows, stride=CHUNK), :] = jax.nn.gelu(y[:, t * LANES:(t + 1) * LANES] + d * us[t])
        z_ref[...] = zs_s[...].astype(BF16)
        return [jnp.concatenate(f[h * nq:(h + 1) * nq], axis=1) for f in (fin_f, fin_b) for h in range(2)]

    @pl.when(p < npp)
    def _():
        zero = jnp.zeros((nbp, half), F32)
        fin = run(nbp, mp, (zero, zero, zero, zero))
        for q in range(4):
            fin_ref[q] = fin[q]

    @pl.when(p >= npp)
    def _():
        run(nbs, ms, tuple(h0_ref[q] for q in range(4)))


def _s5_scan(u, d_skip, layer, wall, qt, a8, h0, tp, seq, ls):
    t, dm = u.shape
    nslab, half = a8.shape[0], a8.shape[-1]
    kc = CHUNK * LANES
    r = min(4096, tp)
    assert tp % r == 0 and (t - tp) % r == 0 and r % seq == 0 and r % ls == 0
    npp, nparts = tp // r, t // r
    nbp, mp, nbs, ms = r // seq, seq // CHUNK, r // ls, ls // CHUNK
    nsp = nparts - npp
    srows = max(nbp * (mp + SUBLANES), nbs * (ms + SUBLANES))
    z, fin = pl.pallas_call(
        functools.partial(_s5_kernel, r=r, npp=npp, nbp=nbp, mp=mp, nbs=nbs, ms=ms),
        out_shape=(jax.ShapeDtypeStruct((t, dm), BF16),
                   jax.ShapeDtypeStruct((npp, 4, nbp, nslab * half), F32)),
        grid=(nslab, nparts),
        in_specs=[pl.BlockSpec((r, LANES), lambda c, p: (p, c)),
                  pl.BlockSpec((None, 1, LANES), lambda c, p: (layer, 0, c)),
                  pl.BlockSpec((None, kc, 3 * kc), lambda c, p: (c, 0, 0)),
                  pl.BlockSpec((None, kc, 2 * kc), lambda c, p: (c, 0, 0)),
                  pl.BlockSpec((None, SUBLANES, half), lambda c, p: (c, 0, 0)),
                  pl.BlockSpec((None, 4, nbs, half), lambda c, p: (jnp.clip(p - npp, 0, nsp - 1), 0, 0, c))],
        out_specs=(pl.BlockSpec((r, LANES), lambda c, p: (p, c)),
                   pl.BlockSpec((None, 4, nbp, half), lambda c, p: (jnp.minimum(p, npp - 1), 0, 0, c))),
        scratch_shapes=[pltpu.VMEM((2 * half // LANES, srows, LANES), F32)] * 4 + [pltpu.VMEM((r, LANES), F32)],
        compiler_params=_params(("arbitrary", "arbitrary")),
        name="s5_scan",
    )(u, d_skip.reshape(d_skip.shape[0], 1, dm), wall, qt, a8, h0)
    return z, fin


def _gmlp_kernel(u_ref, v_ref, g_ref, b_ref, ws_ref, bs_ref, o_ref, *, heads, hd):
    v = _ln(v_ref[...].astype(F32), g_ref[...], b_ref[...]).astype(BF16)
    for h in range(heads):
        sl = slice(h * hd, (h + 1) * hd)
        s = jnp.dot(ws_ref[h].astype(BF16), v[:, sl], preferred_element_type=F32) + bs_ref[:, h:h + 1]
        o_ref[:, sl] = (u_ref[:, sl].astype(F32) * s).astype(BF16)


def _gmlp_gate(z, g_ln_g, g_ln_b, w_s, b_s_t, layer):
    t, d2 = z.shape
    d = d2 // 2
    _, heads, ck, _ = w_s.shape
    return pl.pallas_call(
        functools.partial(_gmlp_kernel, heads=heads, hd=d // heads),
        out_shape=jax.ShapeDtypeStruct((t, d), BF16),
        grid=(t // ck,),
        in_specs=[pl.BlockSpec((ck, d), lambda i: (i, 0)),
                  pl.BlockSpec((ck, d), lambda i: (i, 1)),
                  pl.BlockSpec((None, 1, d), lambda i: (layer, 0, 0)),
                  pl.BlockSpec((None, 1, d), lambda i: (layer, 0, 0)),
                  pl.BlockSpec((None, heads, ck, ck), lambda i: (layer, 0, 0, 0)),
                  pl.BlockSpec((None, ck, heads), lambda i: (layer, 0, 0))],
        out_specs=pl.BlockSpec((ck, d), lambda i: (i, 0)),
        compiler_params=_params(("arbitrary",)),
        name="gmlp_gate",
    )(z, z, g_ln_g.reshape(-1, 1, d), g_ln_b.reshape(-1, 1, d), w_s, b_s_t)


def _second_of4(a, b, c, d):
    return jnp.maximum(jnp.maximum(jnp.minimum(a, b), jnp.minimum(c, d)),
                       jnp.minimum(jnp.maximum(a, b), jnp.maximum(c, d)))


def _first_argmax(vals):
    idx = jnp.zeros(vals[0].shape, I32)
    best = vals[0]
    for k in range(1, len(vals)):
        upd = vals[k] > best
        idx = jnp.where(upd, k, idx)
        best = jnp.where(upd, vals[k], best)
    return idx


def _pick(idx, vals):
    out = vals[0]
    for k in range(1, len(vals)):
        out = jnp.where(idx == k, vals[k], out)
    return out


def _route(h, wr, br):
    n_exp = wr.shape[0]
    epg = n_exp // N_EXPERT_GROUPS
    logits = lax.dot_general(wr, h, (((1,), (1,)), ((), ())), precision=HIGHEST, preferred_element_type=F32)
    ex = jnp.exp(logits - jnp.max(logits, axis=0, keepdims=True))
    probs = ex / jnp.sum(ex, axis=0, keepdims=True)
    sel = probs + br
    srow = [sel[e:e + 1] for e in range(n_exp)]
    prow = [probs[e:e + 1] for e in range(n_exp)]
    scores = []
    for gi in range(N_EXPERT_GROUPS):
        a, b, c, d = srow[gi * epg:(gi + 1) * epg]
        scores.append(jnp.maximum(jnp.maximum(a, b), jnp.maximum(c, d)) + _second_of4(a, b, c, d))
    bg = _first_argmax(scores)
    vs = [_pick(bg, [srow[gi * epg + k] for gi in range(N_EXPERT_GROUPS)]) for k in range(epg)]
    ps = [_pick(bg, [prow[gi * epg + k] for gi in range(N_EXPERT_GROUPS)]) for k in range(epg)]
    i1 = _first_argmax(vs)
    i2 = _first_argmax([jnp.where(i1 == k, -jnp.inf, vs[k]) for k in range(epg)])
    g1, g2 = _pick(i1, ps), _pick(i2, ps)
    tot = g1 + g2
    zero = jnp.zeros_like(g1)
    return jnp.concatenate([(bg * epg + i1).astype(F32), (bg * epg + i2).astype(F32), g1 / tot, g2 / tot,
                            zero, zero, zero, zero], axis=0)


def _postnorm_a_kernel(x_ref, y_ref, gate_ref, g_ref, b_ref, sh_ref, sc_ref, wr_ref, br_ref,
                       x1_ref, hp_ref, route_ref, *, tm, tp, ls, alpha, pitch):
    r = _cond_row(pl.program_id(0), tm, tp, ls)
    v = alpha * x_ref[...] + gate_ref[pl.ds(r, 1), :] * y_ref[...].astype(F32)
    x1 = _ln(v, g_ref[...], b_ref[...])
    x1_ref[...] = x1
    h = x1 * (1 + sc_ref[pl.ds(r, 1), :]) + sh_ref[pl.ds(r, 1), :]
    _store_packed(hp_ref, h, tm, pitch)
    route_ref[...] = _route(h, wr_ref[...], br_ref[...])


def _postnorm_a(x, y, mods, layer, ln_g, ln_b, wr, br, tm, tp, ls, alpha, pitch):
    t, d = x.shape
    n_exp = wr.shape[0]
    row = lambda i: (i, 0)
    lnspec = pl.BlockSpec((None, None, 1, d), lambda i: (layer, 0, 0, 0))
    return pl.pallas_call(
        functools.partial(_postnorm_a_kernel, tm=tm, tp=tp, ls=ls, alpha=alpha, pitch=pitch),
        out_shape=(jax.ShapeDtypeStruct((t, d), F32), jax.ShapeDtypeStruct((t * pitch, LANES), U32),
                   jax.ShapeDtypeStruct((SUBLANES, t), F32)),
        grid=(t // tm,),
        in_specs=[pl.BlockSpec((tm, d), row), pl.BlockSpec((tm, d), row), _mod_spec(d, layer, 2),
                  lnspec, lnspec, _mod_spec(d, layer, 3), _mod_spec(d, layer, 4),
                  pl.BlockSpec((n_exp, d), lambda i: (0, 0)), pl.BlockSpec((n_exp, 1), lambda i: (0, 0))],
        out_specs=(pl.BlockSpec((tm, d), row), pl.BlockSpec((tm * pitch, LANES), row),
                   pl.BlockSpec((SUBLANES, tm), lambda i: (0, i))),
        compiler_params=_params(("arbitrary",)),
        name="postnorm_a",
    )(x, y, mods, ln_g.reshape(ln_g.shape[0], 2, 1, d), ln_b.reshape(ln_b.shape[0], 2, 1, d), mods, mods, wr, br)


def _postnorm_b_kernel(*refs, tm, tp, ls, alpha, pitch, nch, last):
    (pos_cur, pos_nxt, x_ref, ys_ref, cw_ref, gate_ref, g_ref, b_ref) = refs[:8]
    buf, sem = refs[-2:]
    i, n = pl.program_id(0), pl.num_programs(0)
    slot = i % 2
    r = _cond_row(i, tm, tp, ls)

    def issue(pos_ref, sl):
        def body(tk, c):
            for k in range(2):
                a = 2 * tk + k
                pltpu.make_async_copy(ys_ref.at[pos_ref[0, a]],
                                      buf.at[sl, pl.ds(pl.multiple_of(a * pitch, SUBLANES), pitch)],
                                      sem.at[sl]).start(priority=k)
            return c

        lax.fori_loop(0, tm, body, 0)

    @pl.when(i == 0)
    def _():
        issue(pos_cur, 0)

    @pl.when(i + 1 < n)
    def _():
        issue(pos_nxt, 1 - slot)

    def wait_one(_, c):
        pltpu.make_async_copy(ys_ref.at[0], buf.at[slot, pl.ds(0, pitch)], sem.at[slot]).wait()
        return c

    lax.fori_loop(0, 2 * tm, wait_one, 0)

    c0, c1 = cw_ref[:, 0:1], cw_ref[:, 1:2]
    pieces = []
    for s in range(nch):
        lo0, hi0 = _unpack_pair(buf[slot, pl.ds(s, tm, stride=2 * pitch), :])
        lo1, hi1 = _unpack_pair(buf[slot, pl.ds(pitch + s, tm, stride=2 * pitch), :])
        pieces += [c0 * lo0 + c1 * lo1, c0 * hi0 + c1 * hi1]
    f = jnp.concatenate(pieces, axis=1)
    x2 = _ln(alpha * x_ref[...] + gate_ref[pl.ds(r, 1), :] * f, g_ref[...], b_ref[...])
    if last:
        yp_ref, yl_ref = refs[8:10]

        @pl.when(i * tm < tp)
        def _():
            yp_ref[...] = x2

        @pl.when(i * tm >= tp)
        def _():
            yl_ref[...] = x2
    else:
        sh_ref, sc_ref, x2_ref, h_ref = refs[8:12]
        x2_ref[...] = x2
        h_ref[...] = (x2 * (1 + sc_ref[pl.ds(r, 1), :]) + sh_ref[pl.ds(r, 1), :]).astype(BF16)


def _postnorm_b(x, ys3, pos, cw, mods, layer, ln_g, ln_b, tm, tp, ls, alpha, pitch, last):
    t, d = x.shape
    n = t // tm
    row = lambda i: (i, 0)
    lnspec = pl.BlockSpec((None, None, 1, d), lambda i: (layer, 1, 0, 0))
    pos3 = pos.reshape(n, 1, 2 * tm)
    in_specs = [pl.BlockSpec((None, 1, 2 * tm), lambda i: (i, 0, 0), memory_space=pltpu.SMEM),
                pl.BlockSpec((None, 1, 2 * tm), lambda i: (jnp.minimum(i + 1, n - 1), 0, 0), memory_space=pltpu.SMEM),
                pl.BlockSpec((tm, d), row), pl.BlockSpec(memory_space=pl.ANY),
                pl.BlockSpec((tm, 2), row), _mod_spec(d, layer, 5), lnspec, lnspec]
    args = [pos3, pos3, x, ys3, cw, mods, ln_g.reshape(ln_g.shape[0], 2, 1, d), ln_b.reshape(ln_b.shape[0], 2, 1, d)]
    if last:
        npt = tp // tm
        out_shape = (jax.ShapeDtypeStruct((tp, d), F32), jax.ShapeDtypeStruct((t - tp, d), F32))
        out_specs = (pl.BlockSpec((tm, d), lambda i: (jnp.minimum(i, npt - 1), 0)),
                     pl.BlockSpec((tm, d), lambda i: (jnp.maximum(i - npt, 0), 0)))
    else:
        in_specs += [_mod_spec(d, layer + 1, 0), _mod_spec(d, layer + 1, 1)]
        args += [mods, mods]
        out_shape = (jax.ShapeDtypeStruct((t, d), F32), jax.ShapeDtypeStruct((t, d), BF16))
        out_specs = (pl.BlockSpec((tm, d), row), pl.BlockSpec((tm, d), row))
    return pl.pallas_call(
        functools.partial(_postnorm_b_kernel, tm=tm, tp=tp, ls=ls, alpha=alpha, pitch=pitch,
                          nch=d // PAIR, last=last),
        out_shape=out_shape, grid=(n,), in_specs=in_specs, out_specs=out_specs,
        scratch_shapes=[pltpu.VMEM((2, 2 * tm * pitch, LANES), U32), pltpu.SemaphoreType.DMA((2,))],
        compiler_params=_params(("arbitrary",)),
        name="postnorm_b",
    )(*args)


def _plan(route, tme, n_exp):
    t = route.shape[1]
    ef = route[0:2].astype(I32).T.reshape(-1)
    oh = (ef[:, None] == jnp.arange(n_exp, dtype=I32)[None]).astype(I32)
    cs = jnp.cumsum(oh, axis=0)
    counts = cs[-1]
    ntile = (counts + tme - 1) // tme
    tend = jnp.cumsum(ntile)
    tstart = tend - ntile
    pos = jnp.sum(oh * (tstart[None] * tme + cs - 1), axis=1)
    nt = (2 * t) // tme + n_exp
    tid = jnp.arange(nt, dtype=I32)
    texp = jnp.minimum(jnp.sum((tid[:, None] >= tend[None]).astype(I32), axis=1), n_exp - 1)
    first = jnp.concatenate([jnp.ones((1,), I32), (texp[1:] != texp[:-1]).astype(I32)])
    return dict(pos=pos, counts=counts, tstart=tstart, ntile=ntile, texp=texp, first=first,
                nvalid=tend[-1:], nt=nt, cw=route[2:4].T)


ZERO_ROWS = 64


def _scatter_kernel(cnt_ref, tst_ref, ntl_ref, nv_ref, pos_ref, src_ref, dst_ref, zero_s, sem, zsem,
                    *, tms, pitch, tme, n_exp, nt):
    i = pl.program_id(0)

    @pl.when(i == 0)
    def _():
        zero_s[...] = jnp.zeros_like(zero_s)
        per_tile = tme // ZERO_ROWS

        def tail(q, c):
            pltpu.make_async_copy(zero_s, dst_ref.at[pl.ds(q * ZERO_ROWS, ZERO_ROWS)], zsem).start()
            return c

        def tail_wait(_, c):
            pltpu.make_async_copy(zero_s, dst_ref.at[pl.ds(0, ZERO_ROWS)], zsem).wait()
            return c

        lax.fori_loop(nv_ref[0] * per_tile, nt * per_tile, tail, 0)
        lax.fori_loop(nv_ref[0] * per_tile, nt * per_tile, tail_wait, 0)

        def pad(rw, c):
            pltpu.make_async_copy(zero_s.at[0], dst_ref.at[rw], zsem).start()
            return c

        def pad_wait(_, c):
            pltpu.make_async_copy(zero_s.at[0], dst_ref.at[0], zsem).wait()
            return c

        for e in range(n_exp):
            lo = tst_ref[e] * tme + cnt_ref[e]
            hi = (tst_ref[e] + ntl_ref[e]) * tme
            lax.fori_loop(lo, hi, pad, 0)
            lax.fori_loop(lo, hi, pad_wait, 0)

    def move(tk, c):
        src = src_ref.at[pl.ds(pl.multiple_of(tk * pitch, SUBLANES), pitch)]
        for k in range(2):
            pltpu.make_async_copy(src, dst_ref.at[pos_ref[0, 2 * tk + k]], sem).start(priority=k)
        return c

    def wait_one(_, c):
        pltpu.make_async_copy(src_ref.at[pl.ds(0, pitch)], dst_ref.at[0], sem).wait()
        return c

    lax.fori_loop(0, tms, move, 0)
    lax.fori_loop(0, 2 * tms, wait_one, 0)


def _scatter_rows(hp2, plan, tme, n_exp, pitch, tms):
    t = hp2.shape[0] // pitch
    n = t // tms
    assert tme % ZERO_ROWS == 0
    return pl.pallas_call(
        functools.partial(_scatter_kernel, tms=tms, pitch=pitch, tme=tme, n_exp=n_exp, nt=plan["nt"]),
        out_shape=jax.ShapeDtypeStruct((plan["nt"] * tme, pitch, LANES), U32),
        grid_spec=pltpu.PrefetchScalarGridSpec(
            num_scalar_prefetch=4, grid=(n,),
            in_specs=[pl.BlockSpec((None, 1, 2 * tms), lambda i, *_: (i, 0, 0), memory_space=pltpu.SMEM),
                      pl.BlockSpec((tms * pitch, LANES), lambda i, *_: (i, 0))],
            out_specs=pl.BlockSpec(memory_space=pl.ANY),
            scratch_shapes=[pltpu.VMEM((ZERO_ROWS, pitch, LANES), U32), pltpu.SemaphoreType.DMA(()),
                            pltpu.SemaphoreType.DMA(())]),
        compiler_params=pltpu.CompilerParams(dimension_semantics=("arbitrary",), has_side_effects=True,
                                             vmem_limit_bytes=V7X_VMEM_LIMIT),
        name="moe_scatter",
    )(plan["counts"], plan["tstart"], plan["ntile"], plan["nvalid"], plan["pos"].reshape(n, 1, 2 * tms), hp2)


def _moe_a_kernel(te_ref, first_ref, nv_ref, xs_ref, wg_ref, wu_ref, h_ref, lhs_s, wg_s, wu_s,
                  *, tme, nch, pitch):
    i = pl.program_id(1)

    @pl.when(i < nv_ref[0])
    def _():
        @pl.when(first_ref[i] == 1)
        def _():
            wg_s[...] = wg_ref[...].astype(BF16)
            wu_s[...] = wu_ref[...].astype(BF16)

        for s in range(nch):
            lo, hi = _unpack_pair(xs_ref[pl.ds(s, tme, stride=pitch), :])
            lhs_s[:, s * PAIR:s * PAIR + LANES] = lo.astype(BF16)
            lhs_s[:, s * PAIR + LANES:(s + 1) * PAIR] = hi.astype(BF16)
        x = lhs_s[...]
        g = jnp.dot(x, wg_s[...], preferred_element_type=F32)
        u = jnp.dot(x, wu_s[...], preferred_element_type=F32)
        h_ref[...] = (jax.nn.silu(g) * u).astype(BF16)

    @pl.when(i >= nv_ref[0])
    def _():
        h_ref[...] = jnp.zeros_like(h_ref)


def _moe_a(xs2, plan, w_gate, w_up, layer, tme, pitch):
    _, _, d, f = w_gate.shape
    nt = plan["nt"]
    tna = min(512, f)

    def tile(i, nv):
        return jnp.minimum(i, nv[0] - 1)

    wspec = pl.BlockSpec((None, None, d, tna), lambda fi, i, te, fr, nv: (layer, te[tile(i, nv)], 0, fi),
                         pipeline_mode=pl.Buffered(1))
    return pl.pallas_call(
        functools.partial(_moe_a_kernel, tme=tme, nch=d // PAIR, pitch=pitch),
        out_shape=jax.ShapeDtypeStruct((nt * tme, f), BF16),
        grid_spec=pltpu.PrefetchScalarGridSpec(
            num_scalar_prefetch=3, grid=(f // tna, nt),
            in_specs=[pl.BlockSpec((tme * pitch, LANES), lambda fi, i, te, fr, nv: (tile(i, nv), 0)), wspec, wspec],
            out_specs=pl.BlockSpec((tme, tna), lambda fi, i, te, fr, nv: (i, fi)),
            scratch_shapes=[pltpu.VMEM((tme, d), BF16), pltpu.VMEM((d, tna), BF16), pltpu.VMEM((d, tna), BF16)]),
        compiler_params=_params(("arbitrary", "arbitrary")),
        name="moe_up",
    )(plan["texp"], plan["first"], plan["nvalid"], xs2, w_gate, w_up)


def _moe_b_kernel(te_ref, first_ref, nv_ref, h_ref, wd_ref, ys_ref, wd_s, *, tme, pitch):
    i = pl.program_id(0)

    @pl.when(i < nv_ref[0])
    def _():
        @pl.when(first_ref[i] == 1)
        def _():
            wd_s[...] = wd_ref[...].astype(BF16)

        _store_packed(ys_ref, jnp.dot(h_ref[...], wd_s[...], preferred_element_type=F32), tme, pitch)

    @pl.when(i >= nv_ref[0])
    def _():
        ys_ref[...] = jnp.zeros_like(ys_ref)


def _moe_b(hmid, plan, w_down, layer, tme, pitch):
    _, _, f, d = w_down.shape
    nt = plan["nt"]

    def tile(i, nv):
        return jnp.minimum(i, nv[0] - 1)

    return pl.pallas_call(
        functools.partial(_moe_b_kernel, tme=tme, pitch=pitch),
        out_shape=jax.ShapeDtypeStruct((nt * tme * pitch, LANES), U32),
        grid_spec=pltpu.PrefetchScalarGridSpec(
            num_scalar_prefetch=3, grid=(nt,),
            in_specs=[pl.BlockSpec((tme, f), lambda i, te, fr, nv: (tile(i, nv), 0)),
                      pl.BlockSpec((None, None, f, d), lambda i, te, fr, nv: (layer, te[tile(i, nv)], 0, 0),
                                   pipeline_mode=pl.Buffered(1))],
            out_specs=pl.BlockSpec((tme * pitch, LANES), lambda i, te, fr, nv: (i, 0)),
            scratch_shapes=[pltpu.VMEM((f, d), BF16)]),
        compiler_params=_params(("arbitrary",)),
        name="moe_down",
    )(plan["texp"], plan["first"], plan["nvalid"], hmid, w_down)


def _pos2d(length, d):
    rows = length // GRID_W
    r = jnp.repeat(jnp.arange(rows, dtype=F32), GRID_W)
    col = jnp.tile(jnp.arange(GRID_W, dtype=F32), rows)
    nf = d // 4
    omega = 1.0 / (10000.0 ** (jnp.arange(nf, dtype=F32) / nf))

    def enc(p):
        a = p[:, None] * omega[None, :]
        return jnp.concatenate([jnp.sin(a), jnp.cos(a)], axis=-1)

    return jnp.concatenate([enc(r), enc(col)], axis=-1)


def kernel(x_prompt, x_sample, state_ssm_re, state_ssm_im, c, c_ctx, w_ada, b_ada, ln_g, ln_b, s5_w_in, s5_lam_re, s5_lam_im, s5_log_dt, s5_b_re, s5_b_im, s5_c_re, s5_c_im, s5_d, s5_w_glu, s5_w_out, g_w_in, g_b_in, g_ln_g, g_ln_b, g_w_s, g_b_s, g_w_out, w_router, b_router, moe_w_gate, moe_w_up, moe_w_down):
    bp, seq, d = x_prompt.shape
    bs, ls, _ = x_sample.shape
    depth = w_ada.shape[0]
    n_exp = w_router.shape[1]
    tp, ts = bp * seq, bs * ls
    alpha = float((2 * depth) ** 0.25)
    tm = min(256, seq)
    tme = min(512, tp)
    pitch = _pitch(d // PAIR)
    assert bs + 1 <= COND_ROWS and d % PAIR == 0 and seq % tm == 0 and ls % tm == 0

    cond = jnp.concatenate([c_ctx[None].astype(F32), c.astype(F32),
                            jnp.zeros((COND_ROWS - 1 - bs, d), F32)], axis=0)
    mods = _adaln(cond, w_ada, b_ada)
    x, h = _embed(x_prompt.reshape(tp, d), x_sample.reshape(ts, d), _pos2d(ls, d), mods, tm, ls)
    wr, br = w_router.T, b_router.reshape(n_exp, 1)
    g_b_s_t = jnp.swapaxes(g_b_s, -1, -2)

    new_re, new_im = [], []
    for i in range(depth):
        j = i // 2
        if i % 2 == 0:
            wall, qt, a8 = _s5_prep(s5_lam_re[j], s5_lam_im[j], s5_log_dt[j], s5_b_re[j], s5_b_im[j],
                                    s5_c_re[j], s5_c_im[j])
            g, n = s5_lam_re.shape[2:]
            nsp = ts // min(4096, tp)
            h0 = jnp.stack([state_ssm_re[:, j, 0], state_ssm_im[:, j, 0], state_ssm_re[:, j, 1],
                            state_ssm_im[:, j, 1]]).reshape(4, nsp, bs // nsp, g * n).swapaxes(0, 1)
            u, _ = _matmul(h, s5_w_in, (j,), name="s5_in")
            z, fin = _s5_scan(u, s5_d, j, wall, qt, a8, h0, tp, seq, ls)
            fin = fin.swapaxes(0, 1).reshape(4, bp, g, n)
            new_re.append(jnp.stack([fin[0], fin[2]], axis=1))
            new_im.append(jnp.stack([fin[1], fin[3]], axis=1))
            tmm, tnn = min(1024, tp + ts), min(1024, d)
            zg, _ = _matmul(z, s5_w_glu, (j,), epilogue=_epi_glu, extra=(z,),
                            extra_specs=(pl.BlockSpec((tmm, tnn), lambda a, b, k: (a, b)),),
                            out_dtype=BF16, name="s5_glu")
            ymix, _ = _matmul(zg, s5_w_out, (j,), out_dtype=BF16, name="s5_out")
        else:
            tnn = min(1024, 2 * d)
            zz, _ = _matmul(h, g_w_in, (j,), epilogue=_epi_bias_gelu,
                            extra=(g_b_in.reshape(-1, 1, 2 * d),),
                            extra_specs=(pl.BlockSpec((None, 1, tnn), lambda a, b, k: (j, 0, b)),),
                            out_dtype=BF16, name="gmlp_in")
            gated = _gmlp_gate(zz, g_ln_g, g_ln_b, g_w_s, g_b_s_t, j)
            ymix, _ = _matmul(gated, g_w_out, (j,), out_dtype=BF16, name="gmlp_out")
        x, hp, route = _postnorm_a(x, ymix, mods, i, ln_g, ln_b, wr, br, tm, tp, ls, alpha, pitch)
        plan = _plan(route, tme, n_exp)
        xs = _scatter_rows(hp, plan, tme, n_exp, pitch, tm)
        hmid = _moe_a(xs.reshape(-1, LANES), plan, moe_w_gate, moe_w_up, i, tme, pitch)
        ysp = _moe_b(hmid, plan, moe_w_down, i, tme, pitch)
        x, h = _postnorm_b(x, ysp.reshape(-1, pitch, LANES), plan["pos"], plan["cw"], mods, i, ln_g, ln_b,
                           tm, tp, ls, alpha, pitch, last=(i == depth - 1))
    return (x.reshape(bp, seq, d), h.reshape(bs, ls, d),
            jnp.stack(new_re, axis=1).astype(x_prompt.dtype), jnp.stack(new_im, axis=1).astype(x_prompt.dtype))
```

```python
import functools

import jax
import jax.numpy as jnp
from jax import lax
from jax.experimental import pallas as pl
from jax.experimental.pallas import tpu as pltpu

F32, BF16, U32, I32 = jnp.float32, jnp.bfloat16, jnp.uint32, jnp.int32

LN_EPS = 1e-5
GRID_W = 64
N_MOD = 6
N_EXPERT_GROUPS = 4
COND_ROWS = 16
LANES = 128
SUBLANES = 8
CHUNK = SUBLANES
PAIR = 2 * LANES
V7X_VMEM_LIMIT = 56 * 2**20
HIGHEST = lax.Precision.HIGHEST


def _params(sem, vmem=V7X_VMEM_LIMIT):
    return pltpu.CompilerParams(dimension_semantics=sem, vmem_limit_bytes=vmem)


def _cond_row(tile, tm, tp, ls):
    start = tile * tm
    return jnp.where(start < tp, 0, 1 + jnp.maximum(start - tp, 0) // ls)


def _ln(x, g, b):
    mu = jnp.mean(x, axis=-1, keepdims=True)
    xc = x - mu
    var = jnp.mean(xc * xc, axis=-1, keepdims=True)
    return xc * lax.rsqrt(var + LN_EPS) * g + b


def _pitch(nchunks):
    p = -(-nchunks // SUBLANES) * SUBLANES
    if (p // SUBLANES) % 2 == 0:
        p += SUBLANES
    return p


def _pack_pair(a, b):
    ua = lax.bitcast_convert_type(a.astype(BF16).astype(F32), U32)
    ub = lax.bitcast_convert_type(b.astype(BF16).astype(F32), U32)
    return (ua >> 16) | ub


def _unpack_pair(w):
    lo = lax.bitcast_convert_type(w << 16, F32)
    hi = lax.bitcast_convert_type(w & jnp.uint32(0xFFFF0000), F32)
    return lo, hi


def _store_packed(ref, h, rows, pitch):
    nch = h.shape[1] // PAIR
    for s in range(nch):
        ref[pl.ds(s, rows, stride=pitch), :] = _pack_pair(
            h[:, s * PAIR:s * PAIR + LANES], h[:, s * PAIR + LANES:(s + 1) * PAIR])
    for s in range(nch, pitch):
        ref[pl.ds(s, rows, stride=pitch), :] = jnp.zeros((rows, LANES), U32)


def _adaln_kernel(c_ref, w_ref, b_ref, o_ref):
    s = jax.nn.silu(c_ref[...]).astype(BF16)
    o_ref[...] = jnp.dot(s, w_ref[...].astype(BF16), preferred_element_type=F32) + b_ref[...]


def _adaln(cond, w_ada, b_ada):
    depth, d, nm = w_ada.shape
    tn = min(512, nm)
    return pl.pallas_call(
        _adaln_kernel,
        out_shape=jax.ShapeDtypeStruct((depth, COND_ROWS, nm), F32),
        grid=(depth, nm // tn),
        in_specs=[pl.BlockSpec((COND_ROWS, d), lambda i, n: (0, 0)),
                  pl.BlockSpec((None, d, tn), lambda i, n: (i, 0, n)),
                  pl.BlockSpec((None, 1, tn), lambda i, n: (i, 0, n))],
        out_specs=pl.BlockSpec((None, COND_ROWS, tn), lambda i, n: (i, 0, n)),
        compiler_params=_params(("arbitrary", "arbitrary")),
        name="adaln",
    )(cond, w_ada, b_ada.reshape(depth, 1, nm))


def _mod_spec(d, layer, m):
    return pl.BlockSpec((None, COND_ROWS, d), lambda *_: (layer, 0, m))


def _embed_kernel(xp_ref, xs_ref, pos_ref, sh_ref, sc_ref, x_ref, h_ref, *, tm, tp, ls):
    i = pl.program_id(0)
    r = _cond_row(i, tm, tp, ls)
    sh = sh_ref[pl.ds(r, 1), :]
    sc = sc_ref[pl.ds(r, 1), :]

    def emit(x):
        x_ref[...] = x
        h_ref[...] = (x * (1 + sc) + sh).astype(BF16)

    @pl.when(i * tm < tp)
    def _():
        emit(xp_ref[...])

    @pl.when(i * tm >= tp)
    def _():
        emit(xs_ref[...] + pos_ref[...])


def _embed(xp, xs, pos, mods, tm, ls):
    tp, d = xp.shape
    ts = xs.shape[0]
    npt, t = tp // tm, tp + ts
    nps = ls // tm
    return pl.pallas_call(
        functools.partial(_embed_kernel, tm=tm, tp=tp, ls=ls),
        out_shape=(jax.ShapeDtypeStruct((t, d), F32), jax.ShapeDtypeStruct((t, d), BF16)),
        grid=(t // tm,),
        in_specs=[pl.BlockSpec((tm, d), lambda i: (jnp.minimum(i, npt - 1), 0)),
                  pl.BlockSpec((tm, d), lambda i: (jnp.maximum(i - npt, 0), 0)),
                  pl.BlockSpec((tm, d), lambda i: (jnp.maximum(i - npt, 0) % nps, 0)),
                  _mod_spec(d, 0, 0), _mod_spec(d, 0, 1)],
        out_specs=(pl.BlockSpec((tm, d), lambda i: (i, 0)), pl.BlockSpec((tm, d), lambda i: (i, 0))),
        compiler_params=_params(("arbitrary",)),
        name="embed",
    )(xp, xs, pos, mods, mods)


def _epi_store(acc, extra, outs):
    outs[0][...] = acc.astype(outs[0].dtype)


def _epi_bias_gelu(acc, extra, outs):
    outs[0][...] = jax.nn.gelu(acc + extra[0][...]).astype(outs[0].dtype)


def _epi_glu(acc, extra, outs):
    outs[0][...] = (extra[0][...].astype(F32) * jax.nn.sigmoid(acc)).astype(outs[0].dtype)


MM_TM, MM_TN = 512, 1024


def _mm_kernel(*refs, n_extra, epilogue):
    x_ref, w_ref = refs[:2]
    extra, outs, w_s = refs[2:2 + n_extra], refs[2 + n_extra:-1], refs[-1]

    @pl.when(pl.program_id(1) == 0)
    def _():
        w_s[...] = w_ref[...].astype(BF16)

    epilogue(jnp.dot(x_ref[...], w_s[...], preferred_element_type=F32), extra, outs)


def _matmul(x, w, w_prefix, *, epilogue=_epi_store, extra=(), extra_specs=(), out_dtype=F32, name):
    m, kdim = x.shape
    n = w.shape[-1]
    tm, tn = min(MM_TM, m), min(MM_TN, n)
    npre = len(w_prefix)
    return pl.pallas_call(
        functools.partial(_mm_kernel, n_extra=len(extra), epilogue=epilogue),
        out_shape=jax.ShapeDtypeStruct((m, n), out_dtype),
        grid=(n // tn, m // tm),
        in_specs=[pl.BlockSpec((tm, kdim), lambda j, i: (i, 0)),
                  pl.BlockSpec((None,) * npre + (kdim, tn), lambda j, i: tuple(w_prefix) + (0, j),
                               pipeline_mode=pl.Buffered(1)),
                  *extra_specs],
        out_specs=pl.BlockSpec((tm, tn), lambda j, i: (i, j)),
        scratch_shapes=[pltpu.VMEM((kdim, tn), BF16)],
        compiler_params=_params(("arbitrary", "arbitrary")),
        name=name,
    )(x, w, *extra)


def _s5_prep_kernel(lre_ref, lim_ref, ldt_ref, btr_ref, bti_ref, ctr_ref, cti_ref,
                    wall_ref, qt_ref, a8_ref, *, group_ch, state_n):
    half = lre_ref.shape[-1]
    kc = CHUNK * LANES
    mask = (lax.broadcasted_iota(I32, (LANES, half), 0) // group_ch
            == lax.broadcasted_iota(I32, (LANES, half), 1) // state_n)
    tstacks, a8rows = [], []
    for k in range(2):
        lr, li = lre_ref[k], lim_ref[k]
        dt = jnp.exp(ldt_ref[k])
        mag = jnp.exp(lr * dt)
        ar, ai = mag * jnp.cos(li * dt), mag * jnp.sin(li * dt)
        den = lr * lr + li * li
        cr = ((ar - 1) * lr + ai * li) / den
        ci = (ai * lr - (ar - 1) * li) / den
        pw = [(jnp.ones_like(ar), jnp.zeros_like(ar))]
        for _ in range(CHUNK):
            pr, pi = pw[-1]
            pw.append((pr * ar - pi * ai, pr * ai + pi * ar))
        br, bi = btr_ref[k], bti_ref[k]
        crr, cii = ctr_ref[k], cti_ref[k]
        ptiles = []
        for s in range(CHUNK):
            e = CHUNK - 1 - s if k == 0 else s
            zr = cr * pw[e][0] - ci * pw[e][1]
            zi = cr * pw[e][1] + ci * pw[e][0]
            tile = jnp.concatenate([jnp.where(mask, zr * br - zi * bi, 0.0),
                                    jnp.where(mask, zr * bi + zi * br, 0.0)], axis=1)
            ptiles.append(tile)
            wall_ref[s * LANES:(s + 1) * LANES, (1 + k) * kc:(2 + k) * kc] = tile.astype(BF16)
        qt0 = jnp.concatenate([jnp.where(mask, crr, 0.0), jnp.where(mask, -cii, 0.0)], axis=1)
        tstacks.append(lax.dot_general(jnp.concatenate(ptiles, axis=0), qt0, (((1,), (1,)), ((), ())),
                                       precision=HIGHEST, preferred_element_type=F32))
        for t in range(CHUNK):
            e = t + 1 if k == 0 else CHUNK - t
            tile = jnp.concatenate([jnp.where(mask, pw[e][0] * crr - pw[e][1] * cii, 0.0),
                                    jnp.where(mask, -(pw[e][0] * cii + pw[e][1] * crr), 0.0)], axis=1)
            qt_ref[t * LANES:(t + 1) * LANES, k * kc:(k + 1) * kc] = tile.astype(BF16)
        a8rows += [pw[CHUNK][0], pw[CHUNK][1]]
    tf, tb = tstacks
    for s in range(CHUNK):
        for t in range(CHUNK):
            if t > s:
                blk = CHUNK - 1 - (t - s)
                tile = tf[blk * LANES:(blk + 1) * LANES]
            elif s > t:
                tile = tb[(s - t) * LANES:(s - t + 1) * LANES]
            else:
                tile = tf[(CHUNK - 1) * LANES:] + tb[:LANES]
            wall_ref[s * LANES:(s + 1) * LANES, t * LANES:(t + 1) * LANES] = tile.astype(BF16)
    a8_ref[...] = jnp.concatenate(a8rows + [jnp.zeros_like(a8rows[0])] * (SUBLANES - 4), axis=0)


def _s5_prep(lam_re, lam_im, log_dt, b_re, b_im, c_re, c_im):
    _, g, n = lam_re.shape
    gc = b_re.shape[-1]
    gps = LANES // gc
    nslab, half = g // gps, gps * n
    kc = CHUNK * LANES

    def lane(a):
        return a.reshape(2, nslab, 1, half)

    def rows(a):
        return jnp.tile(a.reshape(2, nslab, LANES, n), (1, 1, 1, gps))

    ins = (lane(lam_re), lane(lam_im), lane(jnp.broadcast_to(log_dt[..., None], lam_re.shape)),
           rows(jnp.swapaxes(b_re, -1, -2)), rows(jnp.swapaxes(b_im, -1, -2)), rows(c_re), rows(c_im))
    lspec = pl.BlockSpec((2, None, 1, half), lambda c: (0, c, 0, 0))
    rspec = pl.BlockSpec((2, None, LANES, half), lambda c: (0, c, 0, 0))
    return pl.pallas_call(
        functools.partial(_s5_prep_kernel, group_ch=gc, state_n=n),
        out_shape=(jax.ShapeDtypeStruct((nslab, kc, 3 * kc), BF16),
                   jax.ShapeDtypeStruct((nslab, kc, 2 * kc), BF16),
                   jax.ShapeDtypeStruct((nslab, SUBLANES, half), F32)),
        grid=(nslab,),
        in_specs=[lspec, lspec, lspec, rspec, rspec, rspec, rspec],
        out_specs=(pl.BlockSpec((None, kc, 3 * kc), lambda c: (c, 0, 0)),
                   pl.BlockSpec((None, kc, 2 * kc), lambda c: (c, 0, 0)),
                   pl.BlockSpec((None, SUBLANES, half), lambda c: (c, 0, 0))),
        compiler_params=_params(("arbitrary",)),
        name="s5_prep",
    )(*ins)


def _s5_kernel(u_ref, d_ref, wall_ref, qt_ref, a8_ref, h0_ref, z_ref, fin_ref,
               xf_s, xb_s, sf_s, sb_s, zs_s, *, r, npp, nbp, mp, nbs, ms):
    p = pl.program_id(1)
    rows = r // CHUNK
    kc = CHUNK * LANES
    half = a8_ref.shape[-1]
    us = [u_ref[pl.ds(s, rows, stride=CHUNK), :] for s in range(CHUNK)]
    lhs = jnp.concatenate([x.astype(BF16) for x in us], axis=1)
    big = jnp.dot(lhs, wall_ref[...], preferred_element_type=F32)
    a8 = a8_ref[...]
    afr, afi, abr, abi = a8[0:1], a8[1:2], a8[2:3], a8[3:4]
    d = d_ref[...]

    nq = half // LANES

    def split(a):
        return [a[:, q * LANES:(q + 1) * LANES] for q in range(a.shape[1] // LANES)]

    def run(nb, m, init):
        pitch = m + SUBLANES
        for b in range(nb):
            for q in range(2 * nq):
                xf_s[q, b * pitch:b * pitch + m, :] = big[b * m:(b + 1) * m, kc + q * LANES:kc + (q + 1) * LANES]
                xb_s[q, b * pitch:b * pitch + m, :] = big[b * m:(b + 1) * m,
                                                          2 * kc + q * LANES:2 * kc + (q + 1) * LANES]

        def step(j, st, ar, ai, x_s, s_s):
            new_r, new_i = [], []
            for q in range(nq):
                sr, si = st[q], st[nq + q]
                s_s[q, pl.ds(j, nb, stride=pitch), :] = sr
                s_s[nq + q, pl.ds(j, nb, stride=pitch), :] = si
                xr = x_s[q, pl.ds(j, nb, stride=pitch), :]
                xi = x_s[nq + q, pl.ds(j, nb, stride=pitch), :]
                new_r.append(ar[q] * sr - ai[q] * si + xr)
                new_i.append(ar[q] * si + ai[q] * sr + xi)
            return tuple(new_r + new_i)

        fr, fi, br_, bi_ = split(afr), split(afi), split(abr), split(abi)
        fin_f = lax.fori_loop(0, m, lambda j, st: step(j, st, fr, fi, xf_s, sf_s),
                              tuple(split(init[0]) + split(init[1])))
        fin_b = lax.fori_loop(0, m, lambda jj, st: step(m - 1 - jj, st, br_, bi_, xb_s, sb_s),
                              tuple(split(init[2]) + split(init[3])))
        srows = jnp.concatenate(
            [jnp.concatenate([s_s[q, b * pitch:b * pitch + m, :] for s_s in (sf_s, sb_s) for q in range(2 * nq)],
                             axis=1) for b in range(nb)], axis=0).astype(BF16)
        y = big[:, :kc] + lax.dot_general(srows, qt_ref[...], (((1,), (1,)), ((), ())),
                                          preferred_element_type=F32)
        for t in range(CHUNK):
            zs_s[pl.ds(t, rows, stride=CHUNK), :] = jax.nn.gelu(y[:, t * LANES:(t + 1) * LANES] + d * us[t])
        z_ref[...] = zs_s[...].astype(BF16)
        return [jnp.concatenate(f[h * nq:(h + 1) * nq], axis=1) for f in (fin_f, fin_b) for h in range(2)]

    @pl.when(p < npp)
    def _():
        zero = jnp.zeros((nbp, half), F32)
        fin = run(nbp, mp, (zero, zero, zero, zero))
        for q in range(4):
            fin_ref[q] = fin[q]

    @pl.when(p >= npp)
    def _():
        run(nbs, ms, tuple(h0_ref[q] for q in range(4)))


def _s5_scan(u, d_skip, layer, wall, qt, a8, h0, tp, seq, ls):
    t, dm = u.shape
    nslab, half = a8.shape[0], a8.shape[-1]
    kc = CHUNK * LANES
    r = min(4096, tp)
    assert tp % r == 0 and (t - tp) % r == 0 and r % seq == 0 and r % ls == 0
    npp, nparts = tp // r, t // r
    nbp, mp, nbs, ms = r // seq, seq // CHUNK, r // ls, ls // CHUNK
    nsp = nparts - npp
    srows = max(nbp * (mp + SUBLANES), nbs * (ms + SUBLANES))
    z, fin = pl.pallas_call(
        functools.partial(_s5_kernel, r=r, npp=npp, nbp=nbp, mp=mp, nbs=nbs, ms=ms),
        out_shape=(jax.ShapeDtypeStruct((t, dm), BF16),
                   jax.ShapeDtypeStruct((npp, 4, nbp, nslab * half), F32)),
        grid=(nslab, nparts),
        in_specs=[pl.BlockSpec((r, LANES), lambda c, p: (p, c)),
                  pl.BlockSpec((None, 1, LANES), lambda c, p: (layer, 0, c)),
                  pl.BlockSpec((None, kc, 3 * kc), lambda c, p: (c, 0, 0)),
                  pl.BlockSpec((None, kc, 2 * kc), lambda c, p: (c, 0, 0)),
                  pl.BlockSpec((None, SUBLANES, half), lambda c, p: (c, 0, 0)),
                  pl.BlockSpec((None, 4, nbs, half), lambda c, p: (jnp.clip(p - npp, 0, nsp - 1), 0, 0, c))],
        out_specs=(pl.BlockSpec((r, LANES), lambda c, p: (p, c)),
                   pl.BlockSpec((None, 4, nbp, half), lambda c, p: (jnp.minimum(p, npp - 1), 0, 0, c))),
        scratch_shapes=[pltpu.VMEM((2 * half // LANES, srows, LANES), F32)] * 4 + [pltpu.VMEM((r, LANES), F32)],
        compiler_params=_params(("arbitrary", "arbitrary")),
        name="s5_scan",
    )(u, d_skip.reshape(d_skip.shape[0], 1, dm), wall, qt, a8, h0)
    return z, fin


def _gmlp_kernel(u_ref, v_ref, g_ref, b_ref, ws_ref, bs_ref, o_ref, *, heads, hd):
    v = _ln(v_ref[...].astype(F32), g_ref[...], b_ref[...]).astype(BF16)
    for h in range(heads):
        sl = slice(h * hd, (h + 1) * hd)
        s = jnp.dot(ws_ref[h].astype(BF16), v[:, sl], preferred_element_type=F32) + bs_ref[:, h:h + 1]
        o_ref[:, sl] = (u_ref[:, sl].astype(F32) * s).astype(BF16)


def _gmlp_gate(z, g_ln_g, g_ln_b, w_s, b_s_t, layer):
    t, d2 = z.shape
    d = d2 // 2
    _, heads, ck, _ = w_s.shape
    return pl.pallas_call(
        functools.partial(_gmlp_kernel, heads=heads, hd=d // heads),
        out_shape=jax.ShapeDtypeStruct((t, d), BF16),
        grid=(t // ck,),
        in_specs=[pl.BlockSpec((ck, d), lambda i: (i, 0)),
                  pl.BlockSpec((ck, d), lambda i: (i, 1)),
                  pl.BlockSpec((None, 1, d), lambda i: (layer, 0, 0)),
                  pl.BlockSpec((None, 1, d), lambda i: (layer, 0, 0)),
                  pl.BlockSpec((None, heads, ck, ck), lambda i: (layer, 0, 0, 0)),
                  pl.BlockSpec((None, ck, heads), lambda i: (layer, 0, 0))],
        out_specs=pl.BlockSpec((ck, d), lambda i: (i, 0)),
        compiler_params=_params(("arbitrary",)),
        name="gmlp_gate",
    )(z, z, g_ln_g.reshape(-1, 1, d), g_ln_b.reshape(-1, 1, d), w_s, b_s_t)


def _second_of4(a, b, c, d):
    return jnp.maximum(jnp.maximum(jnp.minimum(a, b), jnp.minimum(c, d)),
                       jnp.minimum(jnp.maximum(a, b), jnp.maximum(c, d)))


def _first_argmax(vals):
    idx = jnp.zeros(vals[0].shape, I32)
    best = vals[0]
    for k in range(1, len(vals)):
        upd = vals[k] > best
        idx = jnp.where(upd, k, idx)
        best = jnp.where(upd, vals[k], best)
    return idx


def _pick(idx, vals):
    out = vals[0]
    for k in range(1, len(vals)):
        out = jnp.where(idx == k, vals[k], out)
    return out


def _route(h, wr, br):
    n_exp = wr.shape[0]
    epg = n_exp // N_EXPERT_GROUPS
    logits = lax.dot_general(wr, h, (((1,), (1,)), ((), ())), precision=HIGHEST, preferred_element_type=F32)
    ex = jnp.exp(logits - jnp.max(logits, axis=0, keepdims=True))
    probs = ex / jnp.sum(ex, axis=0, keepdims=True)
    sel = probs + br
    srow = [sel[e:e + 1] for e in range(n_exp)]
    prow = [probs[e:e + 1] for e in range(n_exp)]
    scores = []
    for gi in range(N_EXPERT_GROUPS):
        a, b, c, d = srow[gi * epg:(gi + 1) * epg]
        scores.append(jnp.maximum(jnp.maximum(a, b), jnp.maximum(c, d)) + _second_of4(a, b, c, d))
    bg = _first_argmax(scores)
    vs = [_pick(bg, [srow[gi * epg + k] for gi in range(N_EXPERT_GROUPS)]) for k in range(epg)]
    ps = [_pick(bg, [prow[gi * epg + k] for gi in range(N_EXPERT_GROUPS)]) for k in range(epg)]
    i1 = _first_argmax(vs)
    i2 = _first_argmax([jnp.where(i1 == k, -jnp.inf, vs[k]) for k in range(epg)])
    g1, g2 = _pick(i1, ps), _pick(i2, ps)
    tot = g1 + g2
    zero = jnp.zeros_like(g1)
    return jnp.concatenate([(bg * epg + i1).astype(F32), (bg * epg + i2).astype(F32), g1 / tot, g2 / tot,
                            zero, zero, zero, zero], axis=0)


def _postnorm_a_kernel(x_ref, y_ref, gate_ref, g_ref, b_ref, sh_ref, sc_ref, wr_ref, br_ref,
                       x1_ref, hp_ref, route_ref, *, tm, tp, ls, alpha, pitch):
    r = _cond_row(pl.program_id(0), tm, tp, ls)
    v = alpha * x_ref[...] + gate_ref[pl.ds(r, 1), :] * y_ref[...].astype(F32)
    x1 = _ln(v, g_ref[...], b_ref[...])
    x1_ref[...] = x1
    h = x1 * (1 + sc_ref[pl.ds(r, 1), :]) + sh_ref[pl.ds(r, 1), :]
    _store_packed(hp_ref, h, tm, pitch)
    route_ref[...] = _route(h, wr_ref[...], br_ref[...])


def _postnorm_a(x, y, mods, layer, ln_g, ln_b, wr, br, tm, tp, ls, alpha, pitch):
    t, d = x.shape
    n_exp = wr.shape[0]
    row = lambda i: (i, 0)
    lnspec = pl.BlockSpec((None, None, 1, d), lambda i: (layer, 0, 0, 0))
    return pl.pallas_call(
        functools.partial(_postnorm_a_kernel, tm=tm, tp=tp, ls=ls, alpha=alpha, pitch=pitch),
        out_shape=(jax.ShapeDtypeStruct((t, d), F32), jax.ShapeDtypeStruct((t * pitch, LANES), U32),
                   jax.ShapeDtypeStruct((SUBLANES, t), F32)),
        grid=(t // tm,),
        in_specs=[pl.BlockSpec((tm, d), row), pl.BlockSpec((tm, d), row), _mod_spec(d, layer, 2),
                  lnspec, lnspec, _mod_spec(d, layer, 3), _mod_spec(d, layer, 4),
                  pl.BlockSpec((n_exp, d), lambda i: (0, 0)), pl.BlockSpec((n_exp, 1), lambda i: (0, 0))],
        out_specs=(pl.BlockSpec((tm, d), row), pl.BlockSpec((tm * pitch, LANES), row),
                   pl.BlockSpec((SUBLANES, tm), lambda i: (0, i))),
        compiler_params=_params(("arbitrary",)),
        name="postnorm_a",
    )(x, y, mods, ln_g.reshape(ln_g.shape[0], 2, 1, d), ln_b.reshape(ln_b.shape[0], 2, 1, d), mods, mods, wr, br)


def _postnorm_b_kernel(*refs, tm, tp, ls, alpha, pitch, nch, last):
    (pos_cur, pos_nxt, x_ref, ys_ref, ys2_ref, cw_ref, gate_ref, g_ref, b_ref) = refs[:9]
    buf, sem = refs[-2:]
    i, n = pl.program_id(0), pl.num_programs(0)
    slot = i % 2
    r = _cond_row(i, tm, tp, ls)

    def issue(pos_ref, sl):
        def body(tk, c):
            for k in range(2):
                a = 2 * tk + k
                pltpu.make_async_copy(ys_ref.at[pos_ref[0, a]],
                                      buf.at[sl, pl.ds(pl.multiple_of(a * pitch, SUBLANES), pitch)],
                                      sem.at[sl]).start(priority=k)
            return c

        lax.fori_loop(0, tm, body, 0, unroll=4)

    @pl.when(i == 0)
    def _():
        issue(pos_cur, 0)

    @pl.when(i + 1 < n)
    def _():
        issue(pos_nxt, 1 - slot)

    pltpu.make_async_copy(ys2_ref.at[pl.ds(0, 2 * tm * pitch)], buf.at[slot], sem.at[slot]).wait()

    c0, c1 = cw_ref[:, 0:1], cw_ref[:, 1:2]
    pieces = []
    for s in range(nch):
        lo0, hi0 = _unpack_pair(buf[slot, pl.ds(s, tm, stride=2 * pitch), :])
        lo1, hi1 = _unpack_pair(buf[slot, pl.ds(pitch + s, tm, stride=2 * pitch), :])
        pieces += [c0 * lo0 + c1 * lo1, c0 * hi0 + c1 * hi1]
    f = jnp.concatenate(pieces, axis=1)
    x2 = _ln(alpha * x_ref[...] + gate_ref[pl.ds(r, 1), :] * f, g_ref[...], b_ref[...])
    if last:
        yp_ref, yl_ref = refs[9:11]

        @pl.when(i * tm < tp)
        def _():
            yp_ref[...] = x2

        @pl.when(i * tm >= tp)
        def _():
            yl_ref[...] = x2
    else:
        sh_ref, sc_ref, x2_ref, h_ref = refs[9:13]
        x2_ref[...] = x2
        h_ref[...] = (x2 * (1 + sc_ref[pl.ds(r, 1), :]) + sh_ref[pl.ds(r, 1), :]).astype(BF16)


def _postnorm_b(x, ys3, pos, cw, mods, layer, ln_g, ln_b, tm, tp, ls, alpha, pitch, last):
    t, d = x.shape
    n = t // tm
    row = lambda i: (i, 0)
    lnspec = pl.BlockSpec((None, None, 1, d), lambda i: (layer, 1, 0, 0))
    pos3 = pos.reshape(n, 1, 2 * tm)
    in_specs = [pl.BlockSpec((None, 1, 2 * tm), lambda i: (i, 0, 0), memory_space=pltpu.SMEM),
                pl.BlockSpec((None, 1, 2 * tm), lambda i: (jnp.minimum(i + 1, n - 1), 0, 0), memory_space=pltpu.SMEM),
                pl.BlockSpec((tm, d), row), pl.BlockSpec(memory_space=pl.ANY), pl.BlockSpec(memory_space=pl.ANY),
                pl.BlockSpec((tm, 2), row), _mod_spec(d, layer, 5), lnspec, lnspec]
    args = [pos3, pos3, x, ys3, ys3.reshape(-1, LANES), cw, mods, ln_g.reshape(ln_g.shape[0], 2, 1, d), ln_b.reshape(ln_b.shape[0], 2, 1, d)]
    if last:
        npt = tp // tm
        out_shape = (jax.ShapeDtypeStruct((tp, d), F32), jax.ShapeDtypeStruct((t - tp, d), F32))
        out_specs = (pl.BlockSpec((tm, d), lambda i: (jnp.minimum(i, npt - 1), 0)),
                     pl.BlockSpec((tm, d), lambda i: (jnp.maximum(i - npt, 0), 0)))
    else:
        in_specs += [_mod_spec(d, layer + 1, 0), _mod_spec(d, layer + 1, 1)]
        args += [mods, mods]
        out_shape = (jax.ShapeDtypeStruct((t, d), F32), jax.ShapeDtypeStruct((t, d), BF16))
        out_specs = (pl.BlockSpec((tm, d), row), pl.BlockSpec((tm, d), row))
    return pl.pallas_call(
        functools.partial(_postnorm_b_kernel, tm=tm, tp=tp, ls=ls, alpha=alpha, pitch=pitch,
                          nch=d // PAIR, last=last),
        out_shape=out_shape, grid=(n,), in_specs=in_specs, out_specs=out_specs,
        scratch_shapes=[pltpu.VMEM((2, 2 * tm * pitch, LANES), U32), pltpu.SemaphoreType.DMA((2,))],
        compiler_params=_params(("arbitrary",)),
        name="postnorm_b",
    )(*args)


def _plan(route, tme, n_exp):
    t = route.shape[1]
    ef = route[0:2].astype(I32).T.reshape(-1)
    oh = (ef[:, None] == jnp.arange(n_exp, dtype=I32)[None]).astype(I32)
    cs = jnp.cumsum(oh, axis=0)
    counts = cs[-1]
    ntile = (counts + tme - 1) // tme
    tend = jnp.cumsum(ntile)
    tstart = tend - ntile
    pos = jnp.sum(oh * (tstart[None] * tme + cs - 1), axis=1)
    nt = (2 * t) // tme + n_exp
    tid = jnp.arange(nt, dtype=I32)
    texp = jnp.minimum(jnp.sum((tid[:, None] >= tend[None]).astype(I32), axis=1), n_exp - 1)
    first = jnp.concatenate([jnp.ones((1,), I32), (texp[1:] != texp[:-1]).astype(I32)])
    after = tend[texp]
    nxt = jnp.where(after < tend[-1], texp[jnp.minimum(after, nt - 1)], -1)
    return dict(pos=pos, counts=counts, tstart=tstart, ntile=ntile, texp=texp, first=first, nxt=nxt,
                nvalid=tend[-1:], nt=nt, cw=route[2:4].T)


ZERO_ROWS = 64


def _scatter_kernel(cnt_ref, tst_ref, ntl_ref, nv_ref, pos_ref, src_ref, dst_ref, zero_s, sem, zsem,
                    *, tms, pitch, tme, n_exp, nt):
    i = pl.program_id(0)

    @pl.when(i == 0)
    def _():
        zero_s[...] = jnp.zeros_like(zero_s)
        per_tile = tme // ZERO_ROWS

        def tail(q, c):
            pltpu.make_async_copy(zero_s, dst_ref.at[pl.ds(q * ZERO_ROWS, ZERO_ROWS)], zsem).start()
            return c

        def tail_wait(_, c):
            pltpu.make_async_copy(zero_s, dst_ref.at[pl.ds(0, ZERO_ROWS)], zsem).wait()
            return c

        lax.fori_loop(nv_ref[0] * per_tile, nt * per_tile, tail, 0)
        lax.fori_loop(nv_ref[0] * per_tile, nt * per_tile, tail_wait, 0)

        def pad(rw, c):
            pltpu.make_async_copy(zero_s.at[0], dst_ref.at[rw], zsem).start()
            return c

        def pad_wait(_, c):
            pltpu.make_async_copy(zero_s.at[0], dst_ref.at[0], zsem).wait()
            return c

        for e in range(n_exp):
            lo = tst_ref[e] * tme + cnt_ref[e]
            hi = (tst_ref[e] + ntl_ref[e]) * tme
            lax.fori_loop(lo, hi, pad, 0)
            lax.fori_loop(lo, hi, pad_wait, 0)

    def move(tk, c):
        for k in range(2):
            pltpu.make_async_copy(src_ref.at[tk], dst_ref.at[pos_ref[0, 2 * tk + k]], sem).start(priority=k)
        return c

    lax.fori_loop(0, tms, move, 0, unroll=4)
    for _ in range(2):
        pltpu.make_async_copy(src_ref, dst_ref.at[pl.ds(0, tms)], sem).wait()


def _scatter_rows(hp3, plan, tme, n_exp, tms):
    t, pitch, _ = hp3.shape
    n = t // tms
    assert tme % ZERO_ROWS == 0
    return pl.pallas_call(
        functools.partial(_scatter_kernel, tms=tms, pitch=pitch, tme=tme, n_exp=n_exp, nt=plan["nt"]),
        out_shape=jax.ShapeDtypeStruct((plan["nt"] * tme, pitch, LANES), U32),
        grid_spec=pltpu.PrefetchScalarGridSpec(
            num_scalar_prefetch=4, grid=(n,),
            in_specs=[pl.BlockSpec((None, 1, 2 * tms), lambda i, *_: (i, 0, 0), memory_space=pltpu.SMEM),
                      pl.BlockSpec((tms, pitch, LANES), lambda i, *_: (i, 0, 0))],
            out_specs=pl.BlockSpec(memory_space=pl.ANY),
            scratch_shapes=[pltpu.VMEM((ZERO_ROWS, pitch, LANES), U32), pltpu.SemaphoreType.DMA(()),
                            pltpu.SemaphoreType.DMA(())]),
        compiler_params=pltpu.CompilerParams(dimension_semantics=("arbitrary",), has_side_effects=True,
                                             vmem_limit_bytes=V7X_VMEM_LIMIT),
        name="moe_scatter",
    )(plan["counts"], plan["tstart"], plan["ntile"], plan["nvalid"], plan["pos"].reshape(n, 1, 2 * tms), hp3)


def _expert_weights(i, first_ref, te_ref, nxt_ref, restart, copies, stages, caches):
    @pl.when(first_ref[i] == 1)
    def _():
        @pl.when(restart)
        def _():
            for cp in copies(te_ref[i]):
                cp.start()

        for cp in copies(te_ref[i]):
            cp.wait()
        for stage, cache in zip(stages, caches):
            cache[...] = stage[...].astype(BF16)

        @pl.when(nxt_ref[i] >= 0)
        def _():
            for cp in copies(nxt_ref[i]):
                cp.start()


def _moe_a_kernel(te_ref, first_ref, nv_ref, nxt_ref, xs_ref, wg_ref, wu_ref, h_ref,
                  lhs_s, wg_s, wu_s, sg_s, su_s, sem, *, layer, tme, tna, nch, pitch):
    fi, i = pl.program_id(0), pl.program_id(1)

    def copies(e):
        cols = pl.ds(pl.multiple_of(fi * tna, LANES), tna)
        return (pltpu.make_async_copy(wg_ref.at[layer, e, :, cols], sg_s, sem.at[0]),
                pltpu.make_async_copy(wu_ref.at[layer, e, :, cols], su_s, sem.at[1]))

    @pl.when(i < nv_ref[0])
    def _():
        _expert_weights(i, first_ref, te_ref, nxt_ref, i == 0, copies, (sg_s, su_s), (wg_s, wu_s))
        for s in range(nch):
            lo, hi = _unpack_pair(xs_ref[pl.ds(s, tme, stride=pitch), :])
            lhs_s[:, s * PAIR:s * PAIR + LANES] = lo.astype(BF16)
            lhs_s[:, s * PAIR + LANES:(s + 1) * PAIR] = hi.astype(BF16)
        x = lhs_s[...]
        g = jnp.dot(x, wg_s[...], preferred_element_type=F32)
        u = jnp.dot(x, wu_s[...], preferred_element_type=F32)
        h_ref[...] = (jax.nn.silu(g) * u).astype(BF16)

    @pl.when(i >= nv_ref[0])
    def _():
        h_ref[...] = jnp.zeros_like(h_ref)


def _moe_a(xs2, plan, w_gate, w_up, layer, tme, pitch):
    _, _, d, f = w_gate.shape
    nt = plan["nt"]
    tna = min(512, f)
    return pl.pallas_call(
        functools.partial(_moe_a_kernel, layer=layer, tme=tme, tna=tna, nch=d // PAIR, pitch=pitch),
        out_shape=jax.ShapeDtypeStruct((nt * tme, f), BF16),
        grid_spec=pltpu.PrefetchScalarGridSpec(
            num_scalar_prefetch=4, grid=(f // tna, nt),
            in_specs=[pl.BlockSpec((tme * pitch, LANES), lambda fi, i, te, fr, nv, nx: (jnp.minimum(i, nv[0] - 1), 0)),
                      pl.BlockSpec(memory_space=pl.ANY), pl.BlockSpec(memory_space=pl.ANY)],
            out_specs=pl.BlockSpec((tme, tna), lambda fi, i, te, fr, nv, nx: (i, fi)),
            scratch_shapes=[pltpu.VMEM((tme, d), BF16), pltpu.VMEM((d, tna), BF16), pltpu.VMEM((d, tna), BF16),
                            pltpu.VMEM((d, tna), F32), pltpu.VMEM((d, tna), F32), pltpu.SemaphoreType.DMA((2,))]),
        compiler_params=_params(("arbitrary", "arbitrary")),
        name="moe_up",
    )(plan["texp"], plan["first"], plan["nvalid"], plan["nxt"], xs2, w_gate, w_up)


def _moe_b_kernel(te_ref, first_ref, nv_ref, nxt_ref, h_ref, wd_ref, ys_ref, wd_s, sd_s, sem,
                  *, layer, tme, pitch):
    i = pl.program_id(0)

    def copies(e):
        return (pltpu.make_async_copy(wd_ref.at[layer, e], sd_s, sem.at[0]),)

    @pl.when(i < nv_ref[0])
    def _():
        _expert_weights(i, first_ref, te_ref, nxt_ref, i == 0, copies, (sd_s,), (wd_s,))
        _store_packed(ys_ref, jnp.dot(h_ref[...], wd_s[...], preferred_element_type=F32), tme, pitch)

    @pl.when(i >= nv_ref[0])
    def _():
        ys_ref[...] = jnp.zeros_like(ys_ref)


def _moe_b(hmid, plan, w_down, layer, tme, pitch):
    _, _, f, d = w_down.shape
    nt = plan["nt"]
    return pl.pallas_call(
        functools.partial(_moe_b_kernel, layer=layer, tme=tme, pitch=pitch),
        out_shape=jax.ShapeDtypeStruct((nt * tme * pitch, LANES), U32),
        grid_spec=pltpu.PrefetchScalarGridSpec(
            num_scalar_prefetch=4, grid=(nt,),
            in_specs=[pl.BlockSpec((tme, f), lambda i, te, fr, nv, nx: (jnp.minimum(i, nv[0] - 1), 0)),
                      pl.BlockSpec(memory_space=pl.ANY)],
            out_specs=pl.BlockSpec((tme * pitch, LANES), lambda i, te, fr, nv, nx: (i, 0)),
            scratch_shapes=[pltpu.VMEM((f, d), BF16), pltpu.VMEM((f, d), F32), pltpu.SemaphoreType.DMA((1,))]),
        compiler_params=_params(("arbitrary",)),
        name="moe_down",
    )(plan["texp"], plan["first"], plan["nvalid"], plan["nxt"], hmid, w_down)


def _pos2d(length, d):
    rows = length // GRID_W
    r = jnp.repeat(jnp.arange(rows, dtype=F32), GRID_W)
    col = jnp.tile(jnp.arange(GRID_W, dtype=F32), rows)
    nf = d // 4
    omega = 1.0 / (10000.0 ** (jnp.arange(nf, dtype=F32) / nf))

    def enc(p):
        a = p[:, None] * omega[None, :]
        return jnp.concatenate([jnp.sin(a), jnp.cos(a)], axis=-1)

    return jnp.concatenate([enc(r), enc(col)], axis=-1)


def kernel(x_prompt, x_sample, state_ssm_re, state_ssm_im, c, c_ctx, w_ada, b_ada, ln_g, ln_b, s5_w_in, s5_lam_re, s5_lam_im, s5_log_dt, s5_b_re, s5_b_im, s5_c_re, s5_c_im, s5_d, s5_w_glu, s5_w_out, g_w_in, g_b_in, g_ln_g, g_ln_b, g_w_s, g_b_s, g_w_out, w_router, b_router, moe_w_gate, moe_w_up, moe_w_down):
    bp, seq, d = x_prompt.shape
    bs, ls, _ = x_sample.shape
    depth = w_ada.shape[0]
    n_exp = w_router.shape[1]
    tp, ts = bp * seq, bs * ls
    alpha = float((2 * depth) ** 0.25)
    tm = min(256, seq)
    tme = min(512, tp)
    pitch = _pitch(d // PAIR)
    assert bs + 1 <= COND_ROWS and d % PAIR == 0 and seq % tm == 0 and ls % tm == 0

    cond = jnp.concatenate([c_ctx[None].astype(F32), c.astype(F32),
                            jnp.zeros((COND_ROWS - 1 - bs, d), F32)], axis=0)
    mods = _adaln(cond, w_ada, b_ada)
    x, h = _embed(x_prompt.reshape(tp, d), x_sample.reshape(ts, d), _pos2d(ls, d), mods, tm, ls)
    wr, br = w_router.T, b_router.reshape(n_exp, 1)
    g_b_s_t = jnp.swapaxes(g_b_s, -1, -2)

    new_re, new_im = [], []
    for i in range(depth):
        j = i // 2
        if i % 2 == 0:
            wall, qt, a8 = _s5_prep(s5_lam_re[j], s5_lam_im[j], s5_log_dt[j], s5_b_re[j], s5_b_im[j],
                                    s5_c_re[j], s5_c_im[j])
            g, n = s5_lam_re.shape[2:]
            nsp = ts // min(4096, tp)
            h0 = jnp.stack([state_ssm_re[:, j, 0], state_ssm_im[:, j, 0], state_ssm_re[:, j, 1],
                            state_ssm_im[:, j, 1]]).reshape(4, nsp, bs // nsp, g * n).swapaxes(0, 1)
            u = _matmul(h, s5_w_in, (j,), name="s5_in")
            z, fin = _s5_scan(u, s5_d, j, wall, qt, a8, h0, tp, seq, ls)
            fin = fin.swapaxes(0, 1).reshape(4, bp, g, n)
            new_re.append(jnp.stack([fin[0], fin[2]], axis=1))
            new_im.append(jnp.stack([fin[1], fin[3]], axis=1))
            tmm, tnn = min(MM_TM, tp + ts), min(MM_TN, d)
            zg = _matmul(z, s5_w_glu, (j,), epilogue=_epi_glu, extra=(z,),
                         extra_specs=(pl.BlockSpec((tmm, tnn), lambda b, a: (a, b)),),
                         out_dtype=BF16, name="s5_glu")
            ymix = _matmul(zg, s5_w_out, (j,), out_dtype=BF16, name="s5_out")
        else:
            tnn = min(MM_TN, 2 * d)
            zz = _matmul(h, g_w_in, (j,), epilogue=_epi_bias_gelu,
                         extra=(g_b_in.reshape(-1, 1, 2 * d),),
                         extra_specs=(pl.BlockSpec((None, 1, tnn), lambda b, a: (j, 0, b)),),
                         out_dtype=BF16, name="gmlp_in")
            gated = _gmlp_gate(zz, g_ln_g, g_ln_b, g_w_s, g_b_s_t, j)
            ymix = _matmul(gated, g_w_out, (j,), out_dtype=BF16, name="gmlp_out")
        x, hp, route = _postnorm_a(x, ymix, mods, i, ln_g, ln_b, wr, br, tm, tp, ls, alpha, pitch)
        plan = _plan(route, tme, n_exp)
        xs = _scatter_rows(hp.reshape(tp + ts, pitch, LANES), plan, tme, n_exp, tm)
        hmid = _moe_a(xs.reshape(-1, LANES), plan, moe_w_gate, moe_w_up, i, tme, pitch)
        ysp = _moe_b(hmid, plan, moe_w_down, i, tme, pitch)
        x, h = _postnorm_b(x, ysp.reshape(-1, pitch, LANES), plan["pos"], plan["cw"], mods, i, ln_g, ln_b,
                           tm, tp, ls, alpha, pitch, last=(i == depth - 1))
    return (x.reshape(bp, seq, d), h.reshape(bs, ls, d),
            jnp.stack(new_re, axis=1).astype(x_prompt.dtype), jnp.stack(new_im, axis=1).astype(x_prompt.dtype))
```

```python
import functools

import jax
import jax.numpy as jnp
from jax import lax
from jax.experimental import pallas as pl
from jax.experimental.pallas import tpu as pltpu

F32, BF16, I32 = jnp.float32, jnp.bfloat16, jnp.int32

LN_EPS = 1e-5
GRID_W = 64
N_MOD = 6
N_EXPERT_GROUPS = 4
COND_ROWS = 16
LANES = 128
SUBLANES = 8
CHUNK = SUBLANES
V7X_VMEM_LIMIT = 56 * 2**20
HIGHEST = lax.Precision.HIGHEST


def _params(sem, vmem=V7X_VMEM_LIMIT):
    return pltpu.CompilerParams(dimension_semantics=sem, vmem_limit_bytes=vmem)


def _cond_row(tile, tm, tp, ls):
    start = tile * tm
    return jnp.where(start < tp, 0, 1 + jnp.maximum(start - tp, 0) // ls)


def _ln(x, g, b):
    mu = jnp.mean(x, axis=-1, keepdims=True)
    xc = x - mu
    var = jnp.mean(xc * xc, axis=-1, keepdims=True)
    return xc * lax.rsqrt(var + LN_EPS) * g + b


def _pitch(nchunks):
    p = -(-nchunks // SUBLANES) * SUBLANES
    if (p // SUBLANES) % 2 == 0:
        p += SUBLANES
    return p


def _store_token_major(ref, h, rows, pitch):
    nch = h.shape[1] // LANES
    for s in range(nch):
        ref[pl.ds(s, rows, stride=pitch), :] = h[:, s * LANES:(s + 1) * LANES]
    for s in range(nch, pitch):
        ref[pl.ds(s, rows, stride=pitch), :] = jnp.zeros((rows, LANES), F32)


def _adaln_kernel(c_ref, w_ref, b_ref, o_ref):
    s = jax.nn.silu(c_ref[...]).astype(BF16)
    o_ref[...] = jnp.dot(s, w_ref[...].astype(BF16), preferred_element_type=F32) + b_ref[...]


def _adaln(cond, w_ada, b_ada):
    depth, d, nm = w_ada.shape
    tn = min(512, nm)
    return pl.pallas_call(
        _adaln_kernel,
        out_shape=jax.ShapeDtypeStruct((depth, COND_ROWS, nm), F32),
        grid=(depth, nm // tn),
        in_specs=[pl.BlockSpec((COND_ROWS, d), lambda i, n: (0, 0)),
                  pl.BlockSpec((None, d, tn), lambda i, n: (i, 0, n)),
                  pl.BlockSpec((None, 1, tn), lambda i, n: (i, 0, n))],
        out_specs=pl.BlockSpec((None, COND_ROWS, tn), lambda i, n: (i, 0, n)),
        compiler_params=_params(("arbitrary", "arbitrary")),
        name="adaln",
    )(cond, w_ada, b_ada.reshape(depth, 1, nm))


def _mod_spec(d, layer, m):
    return pl.BlockSpec((None, COND_ROWS, d), lambda *_: (layer, 0, m))


def _embed_kernel(xp_ref, xs_ref, pos_ref, sh_ref, sc_ref, x_ref, h_ref, *, tm, tp, ls):
    i = pl.program_id(0)
    r = _cond_row(i, tm, tp, ls)
    sh = sh_ref[pl.ds(r, 1), :]
    sc = sc_ref[pl.ds(r, 1), :]

    def emit(x):
        x_ref[...] = x
        h_ref[...] = (x * (1 + sc) + sh).astype(BF16)

    @pl.when(i * tm < tp)
    def _():
        emit(xp_ref[...])

    @pl.when(i * tm >= tp)
    def _():
        emit(xs_ref[...] + pos_ref[...])


def _embed(xp, xs, pos, mods, tm, ls):
    tp, d = xp.shape
    ts = xs.shape[0]
    npt, t = tp // tm, tp + ts
    nps = ls // tm
    return pl.pallas_call(
        functools.partial(_embed_kernel, tm=tm, tp=tp, ls=ls),
        out_shape=(jax.ShapeDtypeStruct((t, d), F32), jax.ShapeDtypeStruct((t, d), BF16)),
        grid=(t // tm,),
        in_specs=[pl.BlockSpec((tm, d), lambda i: (jnp.minimum(i, npt - 1), 0)),
                  pl.BlockSpec((tm, d), lambda i: (jnp.maximum(i - npt, 0), 0)),
                  pl.BlockSpec((tm, d), lambda i: (jnp.maximum(i - npt, 0) % nps, 0)),
                  _mod_spec(d, 0, 0), _mod_spec(d, 0, 1)],
        out_specs=(pl.BlockSpec((tm, d), lambda i: (i, 0)), pl.BlockSpec((tm, d), lambda i: (i, 0))),
        compiler_params=_params(("arbitrary",)),
        name="embed",
    )(xp, xs, pos, mods, mods)


def _epi_store(acc, extra, outs):
    outs[0][...] = acc.astype(outs[0].dtype)


def _epi_bias_gelu(acc, extra, outs):
    outs[0][...] = jax.nn.gelu(acc + extra[0][...]).astype(outs[0].dtype)


def _epi_glu(acc, extra, outs):
    outs[0][...] = (extra[0][...].astype(F32) * jax.nn.sigmoid(acc)).astype(outs[0].dtype)


MM_TM, MM_TN = 512, 1024


def _mm_kernel(*refs, n_extra, epilogue):
    x_ref, w_ref = refs[:2]
    extra, outs, w_s = refs[2:2 + n_extra], refs[2 + n_extra:-1], refs[-1]

    @pl.when(pl.program_id(1) == 0)
    def _():
        w_s[...] = w_ref[...].astype(BF16)

    epilogue(jnp.dot(x_ref[...], w_s[...], preferred_element_type=F32), extra, outs)


def _matmul(x, w, w_prefix, *, epilogue=_epi_store, extra=(), extra_specs=(), out_dtype=F32, name):
    m, kdim = x.shape
    n = w.shape[-1]
    tm, tn = min(MM_TM, m), min(MM_TN, n)
    npre = len(w_prefix)
    return pl.pallas_call(
        functools.partial(_mm_kernel, n_extra=len(extra), epilogue=epilogue),
        out_shape=jax.ShapeDtypeStruct((m, n), out_dtype),
        grid=(n // tn, m // tm),
        in_specs=[pl.BlockSpec((tm, kdim), lambda j, i: (i, 0)),
                  pl.BlockSpec((None,) * npre + (kdim, tn), lambda j, i: tuple(w_prefix) + (0, j),
                               pipeline_mode=pl.Buffered(1)),
                  *extra_specs],
        out_specs=pl.BlockSpec((tm, tn), lambda j, i: (i, j)),
        scratch_shapes=[pltpu.VMEM((kdim, tn), BF16)],
        compiler_params=_params(("arbitrary", "arbitrary")),
        name=name,
    )(x, w, *extra)


def _s5_prep_kernel(lre_ref, lim_ref, ldt_ref, btr_ref, bti_ref, ctr_ref, cti_ref,
                    wp_ref, qt_ref, a8_ref, *, group_ch, state_n):
    half = lre_ref.shape[-1]
    gps = LANES // group_ch
    mask = (lax.broadcasted_iota(I32, (LANES, half), 0) // group_ch
            == lax.broadcasted_iota(I32, (LANES, half), 1) // state_n)
    wp_ref[...] = jnp.zeros_like(wp_ref)
    tstacks, a8rows = [], []
    for k in range(2):
        lr, li = lre_ref[k], lim_ref[k]
        dt = jnp.exp(ldt_ref[k])
        mag = jnp.exp(lr * dt)
        ar, ai = mag * jnp.cos(li * dt), mag * jnp.sin(li * dt)
        den = lr * lr + li * li
        cr = ((ar - 1) * lr + ai * li) / den
        ci = (ai * lr - (ar - 1) * li) / den
        pw = [(jnp.ones_like(ar), jnp.zeros_like(ar))]
        for _ in range(CHUNK):
            pr, pi = pw[-1]
            pw.append((pr * ar - pi * ai, pr * ai + pi * ar))
        br, bi = btr_ref[k], bti_ref[k]
        crr, cii = ctr_ref[k], cti_ref[k]

        def compact(ref, tiles, row_step, col0):
            for g in range(gps):
                q, gl = g // 2, g % 2
                rows = slice(gl * LANES + row_step * group_ch, gl * LANES + (row_step + 1) * group_ch)
                for j, tile in enumerate(tiles):
                    ref[q, rows, col0 + j * LANES:col0 + (j + 1) * LANES] = (
                        tile[g * group_ch:(g + 1) * group_ch, q * LANES:(q + 1) * LANES].astype(BF16))

        def own(tile, g):
            return tile[g * group_ch:(g + 1) * group_ch, (g // 2) * LANES:(g // 2 + 1) * LANES]

        ptiles = []
        for s in range(CHUNK):
            e = CHUNK - 1 - s if k == 0 else s
            zr = cr * pw[e][0] - ci * pw[e][1]
            zi = cr * pw[e][1] + ci * pw[e][0]
            pre, pim = jnp.where(mask, zr * br - zi * bi, 0.0), jnp.where(mask, zr * bi + zi * br, 0.0)
            ptiles.append((pre, pim))
            compact(wp_ref, (pre, pim), s, 2 * LANES * (1 + k))
        c0 = (jnp.where(mask, crr, 0.0), jnp.where(mask, -cii, 0.0))
        per_pair = []
        for q in range(gps // 2):
            p_pair = jnp.concatenate([jnp.concatenate([own(pre, 2 * q + gl), own(pim, 2 * q + gl)], axis=1)
                                      for gl in range(2) for pre, pim in ptiles], axis=0)
            c_pair = jnp.concatenate([jnp.concatenate([own(c0[0], 2 * q + gl), own(c0[1], 2 * q + gl)], axis=1)
                                      for gl in range(2)] + [jnp.zeros((LANES - 2 * group_ch, 2 * LANES), F32)],
                                     axis=0)
            per_pair.append(lax.dot_general(p_pair, c_pair, (((1,), (1,)), ((), ())),
                                            precision=HIGHEST, preferred_element_type=F32))
        tstacks.append(per_pair)
        for t in range(CHUNK):
            e = t + 1 if k == 0 else CHUNK - t
            compact(qt_ref, (jnp.where(mask, pw[e][0] * crr - pw[e][1] * cii, 0.0),
                             jnp.where(mask, -(pw[e][0] * cii + pw[e][1] * crr), 0.0)), t, 2 * LANES * k)
        a8rows += [pw[CHUNK][0], pw[CHUNK][1]]
    tf, tb = tstacks
    blk = group_ch
    for g in range(gps):
        q, gl = g // 2, g % 2
        tfg, tbg = tf[q][gl * LANES:(gl + 1) * LANES], tb[q][gl * LANES:(gl + 1) * LANES]
        m = jnp.zeros((LANES, LANES), F32)
        for t in range(CHUNK):
            shift = ((t - gl) * blk) % LANES
            rf = pltpu.roll(tfg, shift, axis=1) if shift else tfg
            rb = pltpu.roll(tbg, shift, axis=1) if shift else tbg
            up = (CHUNK - 1 - t) * blk
            m = m + (jnp.concatenate([rf[up:], jnp.zeros((up, LANES), F32)], axis=0) if up else rf)
            dn = t * blk
            m = m + (jnp.concatenate([jnp.zeros((dn, LANES), F32), rb[:LANES - dn]], axis=0) if dn else rb)
        wp_ref[q, gl * LANES:(gl + 1) * LANES, gl * LANES:(gl + 1) * LANES] = m.astype(BF16)
    a8_ref[...] = jnp.concatenate(a8rows + [jnp.zeros_like(a8rows[0])] * (SUBLANES - 4), axis=0)


def _s5_prep(lam_re, lam_im, log_dt, b_re, b_im, c_re, c_im):
    _, g, n = lam_re.shape
    gc = b_re.shape[-1]
    gps = LANES // gc
    nslab, half = g // gps, gps * n
    assert gps % 2 == 0 and 2 * n == LANES and gps == CHUNK

    def lane(a):
        return a.reshape(2, nslab, 1, half)

    def rows(a):
        return jnp.tile(a.reshape(2, nslab, LANES, n), (1, 1, 1, gps))

    ins = (lane(lam_re), lane(lam_im), lane(jnp.broadcast_to(log_dt[..., None], lam_re.shape)),
           rows(jnp.swapaxes(b_re, -1, -2)), rows(jnp.swapaxes(b_im, -1, -2)), rows(c_re), rows(c_im))
    lspec = pl.BlockSpec((2, None, 1, half), lambda c: (0, c, 0, 0))
    rspec = pl.BlockSpec((2, None, LANES, half), lambda c: (0, c, 0, 0))
    npair = gps // 2
    return pl.pallas_call(
        functools.partial(_s5_prep_kernel, group_ch=gc, state_n=n),
        out_shape=(jax.ShapeDtypeStruct((nslab, npair, 2 * LANES, 6 * LANES), BF16),
                   jax.ShapeDtypeStruct((nslab, npair, 2 * LANES, 4 * LANES), BF16),
                   jax.ShapeDtypeStruct((nslab, SUBLANES, half), F32)),
        grid=(nslab,),
        in_specs=[lspec, lspec, lspec, rspec, rspec, rspec, rspec],
        out_specs=(pl.BlockSpec((None, npair, 2 * LANES, 6 * LANES), lambda c: (c, 0, 0, 0)),
                   pl.BlockSpec((None, npair, 2 * LANES, 4 * LANES), lambda c: (c, 0, 0, 0)),
                   pl.BlockSpec((None, SUBLANES, half), lambda c: (c, 0, 0))),
        compiler_params=_params(("arbitrary",)),
        name="s5_prep",
    )(*ins)


def _lane_block_transpose(arrs, blk):
    n = len(arrs)
    assert n * blk == LANES
    lane_blk = lax.broadcasted_iota(I32, arrs[0].shape, 1) // blk
    cur, k = list(arrs), n // 2
    while k:
        keep = (lane_blk & k) == 0
        nxt = list(cur)
        for i in range(n):
            if i & k == 0:
                nxt[i] = jnp.where(keep, cur[i], pltpu.roll(cur[i + k], k * blk, axis=1))
                nxt[i + k] = jnp.where(keep, pltpu.roll(cur[i], LANES - k * blk, axis=1), cur[i + k])
        cur, k = nxt, k // 2
    return cur


def _s5_kernel(u_ref, d_ref, wp_ref, qt_ref, a8_ref, h0_ref, z_ref, fin_ref,
               xf_s, xb_s, sf_s, sb_s, zs_s, *, r, npp, nbp, mp, nbs, ms, group_ch):
    p = pl.program_id(1)
    rows = r // CHUNK
    half = a8_ref.shape[-1]
    nq = half // LANES
    us = [u_ref[pl.ds(s, rows, stride=CHUNK), :] for s in range(CHUNK)]
    vs = _lane_block_transpose(us, group_ch)
    bigs = [jnp.dot(jnp.concatenate([vs[2 * q], vs[2 * q + 1]], axis=1).astype(BF16), wp_ref[q],
                    preferred_element_type=F32) for q in range(nq)]
    a8 = a8_ref[...]
    afr, afi, abr, abi = a8[0:1], a8[1:2], a8[2:3], a8[3:4]
    d = d_ref[...]

    def split(a):
        return [a[:, q * LANES:(q + 1) * LANES] for q in range(a.shape[1] // LANES)]

    def run(nb, m, init):
        pitch = m + SUBLANES
        for b in range(nb):
            rb = slice(b * m, (b + 1) * m)
            for q in range(nq):
                for h in range(2):
                    xf_s[h * nq + q, b * pitch:b * pitch + m, :] = bigs[q][rb, (2 + h) * LANES:(3 + h) * LANES]
                    xb_s[h * nq + q, b * pitch:b * pitch + m, :] = bigs[q][rb, (4 + h) * LANES:(5 + h) * LANES]

        def step(j, st, ar, ai, x_s, s_s):
            new_r, new_i = [], []
            for q in range(nq):
                sr, si = st[q], st[nq + q]
                s_s[q, pl.ds(j, nb, stride=pitch), :] = sr
                s_s[nq + q, pl.ds(j, nb, stride=pitch), :] = si
                xr = x_s[q, pl.ds(j, nb, stride=pitch), :]
                xi = x_s[nq + q, pl.ds(j, nb, stride=pitch), :]
                new_r.append(ar[q] * sr - ai[q] * si + xr)
                new_i.append(ar[q] * si + ai[q] * sr + xi)
            return tuple(new_r + new_i)

        fr, fi, br_, bi_ = split(afr), split(afi), split(abr), split(abi)
        fin_f = lax.fori_loop(0, m, lambda j, st: step(j, st, fr, fi, xf_s, sf_s),
                              tuple(split(init[0]) + split(init[1])))
        fin_b = lax.fori_loop(0, m, lambda jj, st: step(m - 1 - jj, st, br_, bi_, xb_s, sb_s),
                              tuple(split(init[2]) + split(init[3])))
        yg = []
        for q in range(nq):
            sq = jnp.concatenate(
                [jnp.concatenate([s_s[h * nq + q, b * pitch:b * pitch + m, :] for s_s in (sf_s, sb_s)
                                  for h in range(2)], axis=1) for b in range(nb)], axis=0).astype(BF16)
            yi = lax.dot_general(sq, qt_ref[q], (((1,), (1,)), ((), ())), preferred_element_type=F32)
            yg += [bigs[q][:, :LANES] + yi[:, :LANES], bigs[q][:, LANES:2 * LANES] + yi[:, LANES:]]
        ys = _lane_block_transpose(yg, group_ch)
        for t in range(CHUNK):
            zs_s[pl.ds(t, rows, stride=CHUNK), :] = jax.nn.gelu(ys[t] + d * us[t])
        z_ref[...] = zs_s[...].astype(BF16)
        return [jnp.concatenate(f[h * nq:(h + 1) * nq], axis=1) for f in (fin_f, fin_b) for h in range(2)]

    @pl.when(p < npp)
    def _():
        zero = jnp.zeros((nbp, half), F32)
        fin = run(nbp, mp, (zero, zero, zero, zero))
        for q in range(4):
            fin_ref[q] = fin[q]

    @pl.when(p >= npp)
    def _():
        run(nbs, ms, tuple(h0_ref[q] for q in range(4)))


def _s5_scan(u, d_skip, layer, wp, qt, a8, h0, tp, seq, ls, group_ch):
    t, dm = u.shape
    nslab, half = a8.shape[0], a8.shape[-1]
    r = min(4096, tp)
    assert tp % r == 0 and (t - tp) % r == 0 and r % seq == 0 and r % ls == 0
    npp, nparts = tp // r, t // r
    nbp, mp, nbs, ms = r // seq, seq // CHUNK, r // ls, ls // CHUNK
    nsp = nparts - npp
    srows = max(nbp * (mp + SUBLANES), nbs * (ms + SUBLANES))
    z, fin = pl.pallas_call(
        functools.partial(_s5_kernel, r=r, npp=npp, nbp=nbp, mp=mp, nbs=nbs, ms=ms, group_ch=group_ch),
        out_shape=(jax.ShapeDtypeStruct((t, dm), BF16),
                   jax.ShapeDtypeStruct((npp, 4, nbp, nslab * half), F32)),
        grid=(nslab, nparts),
        in_specs=[pl.BlockSpec((r, LANES), lambda c, p: (p, c)),
                  pl.BlockSpec((None, 1, LANES), lambda c, p: (layer, 0, c)),
                  pl.BlockSpec((None,) + wp.shape[1:], lambda c, p: (c, 0, 0, 0)),
                  pl.BlockSpec((None,) + qt.shape[1:], lambda c, p: (c, 0, 0, 0)),
                  pl.BlockSpec((None, SUBLANES, half), lambda c, p: (c, 0, 0)),
                  pl.BlockSpec((None, 4, nbs, half), lambda c, p: (jnp.clip(p - npp, 0, nsp - 1), 0, 0, c))],
        out_specs=(pl.BlockSpec((r, LANES), lambda c, p: (p, c)),
                   pl.BlockSpec((None, 4, nbp, half), lambda c, p: (jnp.minimum(p, npp - 1), 0, 0, c))),
        scratch_shapes=[pltpu.VMEM((2 * half // LANES, srows, LANES), F32)] * 4 + [pltpu.VMEM((r, LANES), F32)],
        compiler_params=_params(("arbitrary", "arbitrary")),
        name="s5_scan",
    )(u, d_skip.reshape(d_skip.shape[0], 1, dm), wp, qt, a8, h0)
    return z, fin


def _gmlp_kernel(u_ref, v_ref, g_ref, b_ref, ws_ref, bs_ref, o_ref, *, heads, hd):
    v = _ln(v_ref[...].astype(F32), g_ref[...], b_ref[...]).astype(BF16)
    for h in range(heads):
        sl = slice(h * hd, (h + 1) * hd)
        s = jnp.dot(ws_ref[h].astype(BF16), v[:, sl], preferred_element_type=F32) + bs_ref[:, h:h + 1]
        o_ref[:, sl] = (u_ref[:, sl].astype(F32) * s).astype(BF16)


def _gmlp_gate(z, g_ln_g, g_ln_b, w_s, b_s_t, layer):
    t, d2 = z.shape
    d = d2 // 2
    _, heads, ck, _ = w_s.shape
    return pl.pallas_call(
        functools.partial(_gmlp_kernel, heads=heads, hd=d // heads),
        out_shape=jax.ShapeDtypeStruct((t, d), BF16),
        grid=(t // ck,),
        in_specs=[pl.BlockSpec((ck, d), lambda i: (i, 0)),
                  pl.BlockSpec((ck, d), lambda i: (i, 1)),
                  pl.BlockSpec((None, 1, d), lambda i: (layer, 0, 0)),
                  pl.BlockSpec((None, 1, d), lambda i: (layer, 0, 0)),
                  pl.BlockSpec((None, heads, ck, ck), lambda i: (layer, 0, 0, 0)),
                  pl.BlockSpec((None, ck, heads), lambda i: (layer, 0, 0))],
        out_specs=pl.BlockSpec((ck, d), lambda i: (i, 0)),
        compiler_params=_params(("arbitrary",)),
        name="gmlp_gate",
    )(z, z, g_ln_g.reshape(-1, 1, d), g_ln_b.reshape(-1, 1, d), w_s, b_s_t)


def _second_of4(a, b, c, d):
    return jnp.maximum(jnp.maximum(jnp.minimum(a, b), jnp.minimum(c, d)),
                       jnp.minimum(jnp.maximum(a, b), jnp.maximum(c, d)))


def _first_argmax(vals):
    idx = jnp.zeros(vals[0].shape, I32)
    best = vals[0]
    for k in range(1, len(vals)):
        upd = vals[k] > best
        idx = jnp.where(upd, k, idx)
        best = jnp.where(upd, vals[k], best)
    return idx


def _pick(idx, vals):
    out = vals[0]
    for k in range(1, len(vals)):
        out = jnp.where(idx == k, vals[k], out)
    return out


def _route(h, wr, br):
    n_exp = wr.shape[0]
    epg = n_exp // N_EXPERT_GROUPS
    logits = lax.dot_general(wr, h, (((1,), (1,)), ((), ())), precision=HIGHEST, preferred_element_type=F32)
    ex = jnp.exp(logits - jnp.max(logits, axis=0, keepdims=True))
    probs = ex / jnp.sum(ex, axis=0, keepdims=True)
    sel = probs + br
    srow = [sel[e:e + 1] for e in range(n_exp)]
    prow = [probs[e:e + 1] for e in range(n_exp)]
    scores = []
    for gi in range(N_EXPERT_GROUPS):
        a, b, c, d = srow[gi * epg:(gi + 1) * epg]
        scores.append(jnp.maximum(jnp.maximum(a, b), jnp.maximum(c, d)) + _second_of4(a, b, c, d))
    bg = _first_argmax(scores)
    vs = [_pick(bg, [srow[gi * epg + k] for gi in range(N_EXPERT_GROUPS)]) for k in range(epg)]
    ps = [_pick(bg, [prow[gi * epg + k] for gi in range(N_EXPERT_GROUPS)]) for k in range(epg)]
    i1 = _first_argmax(vs)
    i2 = _first_argmax([jnp.where(i1 == k, -jnp.inf, vs[k]) for k in range(epg)])
    g1, g2 = _pick(i1, ps), _pick(i2, ps)
    tot = g1 + g2
    zero = jnp.zeros_like(g1)
    return jnp.concatenate([(bg * epg + i1).astype(F32), (bg * epg + i2).astype(F32), g1 / tot, g2 / tot,
                            zero, zero, zero, zero], axis=0)


def _postnorm_a_kernel(x_ref, y_ref, gate_ref, g_ref, b_ref, sh_ref, sc_ref, wr_ref, br_ref,
                       x1_ref, hp_ref, route_ref, *, tm, tp, ls, alpha, pitch):
    r = _cond_row(pl.program_id(0), tm, tp, ls)
    v = alpha * x_ref[...] + gate_ref[pl.ds(r, 1), :] * y_ref[...].astype(F32)
    x1 = _ln(v, g_ref[...], b_ref[...])
    x1_ref[...] = x1
    h = x1 * (1 + sc_ref[pl.ds(r, 1), :]) + sh_ref[pl.ds(r, 1), :]
    _store_token_major(hp_ref, h, tm, pitch)
    route_ref[...] = _route(h, wr_ref[...], br_ref[...])


def _postnorm_a(x, y, mods, layer, ln_g, ln_b, wr, br, tm, tp, ls, alpha, pitch):
    t, d = x.shape
    n_exp = wr.shape[0]
    row = lambda i: (i, 0)
    lnspec = pl.BlockSpec((None, None, 1, d), lambda i: (layer, 0, 0, 0))
    return pl.pallas_call(
        functools.partial(_postnorm_a_kernel, tm=tm, tp=tp, ls=ls, alpha=alpha, pitch=pitch),
        out_shape=(jax.ShapeDtypeStruct((t, d), F32), jax.ShapeDtypeStruct((t * pitch, LANES), F32),
                   jax.ShapeDtypeStruct((SUBLANES, t), F32)),
        grid=(t // tm,),
        in_specs=[pl.BlockSpec((tm, d), row), pl.BlockSpec((tm, d), row), _mod_spec(d, layer, 2),
                  lnspec, lnspec, _mod_spec(d, layer, 3), _mod_spec(d, layer, 4),
                  pl.BlockSpec((n_exp, d), lambda i: (0, 0)), pl.BlockSpec((n_exp, 1), lambda i: (0, 0))],
        out_specs=(pl.BlockSpec((tm, d), row), pl.BlockSpec((tm * pitch, LANES), row),
                   pl.BlockSpec((SUBLANES, tm), lambda i: (0, i))),
        compiler_params=_params(("arbitrary",)),
        name="postnorm_a",
    )(x, y, mods, ln_g.reshape(ln_g.shape[0], 2, 1, d), ln_b.reshape(ln_b.shape[0], 2, 1, d), mods, mods, wr, br)


def _postnorm_b_kernel(*refs, tm, tp, ls, alpha, pitch, nch, last):
    (pos_cur, pos_nxt, x_ref, ys_ref, ys2_ref, cw_ref, gate_ref, g_ref, b_ref) = refs[:9]
    buf, sem = refs[-2:]
    i, n = pl.program_id(0), pl.num_programs(0)
    slot = i % 2
    r = _cond_row(i, tm, tp, ls)

    def issue(pos_ref, sl):
        def body(tk, c):
            for k in range(2):
                pltpu.make_async_copy(ys_ref.at[pos_ref[0, 2 * tk + k]],
                                      buf.at[sl, pl.ds(pl.multiple_of((k * tm + tk) * pitch, SUBLANES), pitch)],
                                      sem.at[sl]).start(priority=k)
            return c

        lax.fori_loop(0, tm, body, 0, unroll=4)

    @pl.when(i == 0)
    def _():
        issue(pos_cur, 0)

    @pl.when(i + 1 < n)
    def _():
        issue(pos_nxt, 1 - slot)

    pltpu.make_async_copy(ys2_ref.at[pl.ds(0, 2 * tm * pitch)], buf.at[slot], sem.at[slot]).wait()

    c0, c1 = cw_ref[:, 0:1], cw_ref[:, 1:2]
    f = jnp.concatenate([c0 * buf[slot, pl.ds(s, tm, stride=pitch), :]
                         + c1 * buf[slot, pl.ds(tm * pitch + s, tm, stride=pitch), :] for s in range(nch)], axis=1)
    x2 = _ln(alpha * x_ref[...] + gate_ref[pl.ds(r, 1), :] * f, g_ref[...], b_ref[...])
    if last:
        yp_ref, yl_ref = refs[9:11]

        @pl.when(i * tm < tp)
        def _():
            yp_ref[...] = x2

        @pl.when(i * tm >= tp)
        def _():
            yl_ref[...] = x2
    else:
        sh_ref, sc_ref, x2_ref, h_ref = refs[9:13]
        x2_ref[...] = x2
        h_ref[...] = (x2 * (1 + sc_ref[pl.ds(r, 1), :]) + sh_ref[pl.ds(r, 1), :]).astype(BF16)


def _postnorm_b(x, ys3, pos, cw, mods, layer, ln_g, ln_b, tm, tp, ls, alpha, pitch, last):
    t, d = x.shape
    n = t // tm
    row = lambda i: (i, 0)
    lnspec = pl.BlockSpec((None, None, 1, d), lambda i: (layer, 1, 0, 0))
    pos3 = pos.reshape(n, 1, 2 * tm)
    in_specs = [pl.BlockSpec((None, 1, 2 * tm), lambda i: (i, 0, 0), memory_space=pltpu.SMEM),
                pl.BlockSpec((None, 1, 2 * tm), lambda i: (jnp.minimum(i + 1, n - 1), 0, 0), memory_space=pltpu.SMEM),
                pl.BlockSpec((tm, d), row), pl.BlockSpec(memory_space=pl.ANY), pl.BlockSpec(memory_space=pl.ANY),
                pl.BlockSpec((tm, 2), row), _mod_spec(d, layer, 5), lnspec, lnspec]
    args = [pos3, pos3, x, ys3, ys3.reshape(-1, LANES), cw, mods, ln_g.reshape(ln_g.shape[0], 2, 1, d), ln_b.reshape(ln_b.shape[0], 2, 1, d)]
    if last:
        npt = tp // tm
        out_shape = (jax.ShapeDtypeStruct((tp, d), F32), jax.ShapeDtypeStruct((t - tp, d), F32))
        out_specs = (pl.BlockSpec((tm, d), lambda i: (jnp.minimum(i, npt - 1), 0)),
                     pl.BlockSpec((tm, d), lambda i: (jnp.maximum(i - npt, 0), 0)))
    else:
        in_specs += [_mod_spec(d, layer + 1, 0), _mod_spec(d, layer + 1, 1)]
        args += [mods, mods]
        out_shape = (jax.ShapeDtypeStruct((t, d), F32), jax.ShapeDtypeStruct((t, d), BF16))
        out_specs = (pl.BlockSpec((tm, d), row), pl.BlockSpec((tm, d), row))
    return pl.pallas_call(
        functools.partial(_postnorm_b_kernel, tm=tm, tp=tp, ls=ls, alpha=alpha, pitch=pitch,
                          nch=d // LANES, last=last),
        out_shape=out_shape, grid=(n,), in_specs=in_specs, out_specs=out_specs,
        scratch_shapes=[pltpu.VMEM((2, 2 * tm * pitch, LANES), F32), pltpu.SemaphoreType.DMA((2,))],
        compiler_params=_params(("arbitrary",)),
        name="postnorm_b",
    )(*args)


def _plan(route, tme, n_exp):
    t = route.shape[1]
    ef = route[0:2].astype(I32).T.reshape(-1)
    oh = (ef[:, None] == jnp.arange(n_exp, dtype=I32)[None]).astype(I32)
    cs = jnp.cumsum(oh, axis=0)
    counts = cs[-1]
    ntile = (counts + tme - 1) // tme
    tend = jnp.cumsum(ntile)
    tstart = tend - ntile
    pos = jnp.sum(oh * (tstart[None] * tme + cs - 1), axis=1)
    nt = (2 * t) // tme + n_exp
    tid = jnp.arange(nt, dtype=I32)
    texp = jnp.minimum(jnp.sum((tid[:, None] >= tend[None]).astype(I32), axis=1), n_exp - 1)
    first = jnp.concatenate([jnp.ones((1,), I32), (texp[1:] != texp[:-1]).astype(I32)])
    after = tend[texp]
    nxt = jnp.where(after < tend[-1], texp[jnp.minimum(after, nt - 1)], -1)
    return dict(pos=pos, counts=counts, tstart=tstart, ntile=ntile, texp=texp, first=first, nxt=nxt,
                nvalid=tend[-1:], nt=nt, cw=route[2:4].T)


ZERO_ROWS = 64


def _scatter_kernel(cnt_ref, tst_ref, ntl_ref, nv_ref, pos_ref, src_ref, dst_ref, zero_s, sem, zsem,
                    *, tms, pitch, tme, n_exp, nt):
    i = pl.program_id(0)

    @pl.when(i == 0)
    def _():
        zero_s[...] = jnp.zeros_like(zero_s)
        per_tile = tme // ZERO_ROWS

        def tail(q, c):
            pltpu.make_async_copy(zero_s, dst_ref.at[pl.ds(q * ZERO_ROWS, ZERO_ROWS)], zsem).start()
            return c

        def tail_wait(_, c):
            pltpu.make_async_copy(zero_s, dst_ref.at[pl.ds(0, ZERO_ROWS)], zsem).wait()
            return c

        lax.fori_loop(nv_ref[0] * per_tile, nt * per_tile, tail, 0)
        lax.fori_loop(nv_ref[0] * per_tile, nt * per_tile, tail_wait, 0)

        def pad(rw, c):
            pltpu.make_async_copy(zero_s.at[0], dst_ref.at[rw], zsem).start()
            return c

        def pad_wait(_, c):
            pltpu.make_async_copy(zero_s.at[0], dst_ref.at[0], zsem).wait()
            return c

        for e in range(n_exp):
            lo = tst_ref[e] * tme + cnt_ref[e]
            hi = (tst_ref[e] + ntl_ref[e]) * tme
            lax.fori_loop(lo, hi, pad, 0)
            lax.fori_loop(lo, hi, pad_wait, 0)

    def move(tk, c):
        for k in range(2):
            pltpu.make_async_copy(src_ref.at[tk], dst_ref.at[pos_ref[0, 2 * tk + k]], sem).start(priority=k)
        return c

    lax.fori_loop(0, tms, move, 0, unroll=4)
    for _ in range(2):
        pltpu.make_async_copy(src_ref, dst_ref.at[pl.ds(0, tms)], sem).wait()


def _scatter_rows(hp3, plan, tme, n_exp, tms):
    t, pitch, _ = hp3.shape
    n = t // tms
    assert tme % ZERO_ROWS == 0
    return pl.pallas_call(
        functools.partial(_scatter_kernel, tms=tms, pitch=pitch, tme=tme, n_exp=n_exp, nt=plan["nt"]),
        out_shape=jax.ShapeDtypeStruct((plan["nt"] * tme, pitch, LANES), F32),
        grid_spec=pltpu.PrefetchScalarGridSpec(
            num_scalar_prefetch=4, grid=(n,),
            in_specs=[pl.BlockSpec((None, 1, 2 * tms), lambda i, *_: (i, 0, 0), memory_space=pltpu.SMEM),
                      pl.BlockSpec((tms, pitch, LANES), lambda i, *_: (i, 0, 0))],
            out_specs=pl.BlockSpec(memory_space=pl.ANY),
            scratch_shapes=[pltpu.VMEM((ZERO_ROWS, pitch, LANES), F32), pltpu.SemaphoreType.DMA(()),
                            pltpu.SemaphoreType.DMA(())]),
        compiler_params=pltpu.CompilerParams(dimension_semantics=("arbitrary",), has_side_effects=True,
                                             vmem_limit_bytes=V7X_VMEM_LIMIT),
        name="moe_scatter",
    )(plan["counts"], plan["tstart"], plan["ntile"], plan["nvalid"], plan["pos"].reshape(n, 1, 2 * tms), hp3)


def _expert_weights(i, first_ref, te_ref, nxt_ref, restart, copies, stages, caches):
    @pl.when(first_ref[i] == 1)
    def _():
        @pl.when(restart)
        def _():
            for cp in copies(te_ref[i]):
                cp.start()

        for cp in copies(te_ref[i]):
            cp.wait()
        for stage, cache in zip(stages, caches):
            cache[...] = stage[...].astype(BF16)

        @pl.when(nxt_ref[i] >= 0)
        def _():
            for cp in copies(nxt_ref[i]):
                cp.start()


def _moe_a_kernel(te_ref, first_ref, nv_ref, nxt_ref, xs_ref, wg_ref, wu_ref, h_ref,
                  lhs_s, wg_s, wu_s, sg_s, su_s, sem, *, layer, tme, tna, nch, pitch):
    fi, i = pl.program_id(0), pl.program_id(1)

    def copies(e):
        cols = pl.ds(pl.multiple_of(fi * tna, LANES), tna)
        return (pltpu.make_async_copy(wg_ref.at[layer, e, :, cols], sg_s, sem.at[0]),
                pltpu.make_async_copy(wu_ref.at[layer, e, :, cols], su_s, sem.at[1]))

    @pl.when(i < nv_ref[0])
    def _():
        _expert_weights(i, first_ref, te_ref, nxt_ref, i == 0, copies, (sg_s, su_s), (wg_s, wu_s))
        for s in range(nch):
            lhs_s[:, s * LANES:(s + 1) * LANES] = xs_ref[pl.ds(s, tme, stride=pitch), :].astype(BF16)
        x = lhs_s[...]
        g = jnp.dot(x, wg_s[...], preferred_element_type=F32)
        u = jnp.dot(x, wu_s[...], preferred_element_type=F32)
        h_ref[...] = (jax.nn.silu(g) * u).astype(BF16)

    @pl.when(i >= nv_ref[0])
    def _():
        h_ref[...] = jnp.zeros_like(h_ref)


def _moe_a(xs2, plan, w_gate, w_up, layer, tme, pitch):
    _, _, d, f = w_gate.shape
    nt = plan["nt"]
    tna = min(512, f)
    return pl.pallas_call(
        functools.partial(_moe_a_kernel, layer=layer, tme=tme, tna=tna, nch=d // LANES, pitch=pitch),
        out_shape=jax.ShapeDtypeStruct((nt * tme, f), BF16),
        grid_spec=pltpu.PrefetchScalarGridSpec(
            num_scalar_prefetch=4, grid=(f // tna, nt),
            in_specs=[pl.BlockSpec((tme * pitch, LANES), lambda fi, i, te, fr, nv, nx: (jnp.minimum(i, nv[0] - 1), 0)),
                      pl.BlockSpec(memory_space=pl.ANY), pl.BlockSpec(memory_space=pl.ANY)],
            out_specs=pl.BlockSpec((tme, tna), lambda fi, i, te, fr, nv, nx: (i, fi)),
            scratch_shapes=[pltpu.VMEM((tme, d), BF16), pltpu.VMEM((d, tna), BF16), pltpu.VMEM((d, tna), BF16),
                            pltpu.VMEM((d, tna), F32), pltpu.VMEM((d, tna), F32), pltpu.SemaphoreType.DMA((2,))]),
        compiler_params=_params(("arbitrary", "arbitrary"), vmem=V7X_VMEM_LIMIT + 4 * 2**20),
        name="moe_up",
    )(plan["texp"], plan["first"], plan["nvalid"], plan["nxt"], xs2, w_gate, w_up)


def _moe_b_kernel(te_ref, first_ref, nv_ref, nxt_ref, h_ref, wd_ref, ys_ref, wd_s, sd_s, sem,
                  *, layer, tme, pitch):
    i = pl.program_id(0)

    def copies(e):
        return (pltpu.make_async_copy(wd_ref.at[layer, e], sd_s, sem.at[0]),)

    @pl.when(i < nv_ref[0])
    def _():
        _expert_weights(i, first_ref, te_ref, nxt_ref, i == 0, copies, (sd_s,), (wd_s,))
        _store_token_major(ys_ref, jnp.dot(h_ref[...], wd_s[...], preferred_element_type=F32), tme, pitch)

    @pl.when(i >= nv_ref[0])
    def _():
        ys_ref[...] = jnp.zeros_like(ys_ref)


def _moe_b(hmid, plan, w_down, layer, tme, pitch):
    _, _, f, d = w_down.shape
    nt = plan["nt"]
    return pl.pallas_call(
        functools.partial(_moe_b_kernel, layer=layer, tme=tme, pitch=pitch),
        out_shape=jax.ShapeDtypeStruct((nt * tme * pitch, LANES), F32),
        grid_spec=pltpu.PrefetchScalarGridSpec(
            num_scalar_prefetch=4, grid=(nt,),
            in_specs=[pl.BlockSpec((tme, f), lambda i, te, fr, nv, nx: (jnp.minimum(i, nv[0] - 1), 0)),
                      pl.BlockSpec(memory_space=pl.ANY)],
            out_specs=pl.BlockSpec((tme * pitch, LANES), lambda i, te, fr, nv, nx: (i, 0)),
            scratch_shapes=[pltpu.VMEM((f, d), BF16), pltpu.VMEM((f, d), F32), pltpu.SemaphoreType.DMA((1,))]),
        compiler_params=_params(("arbitrary",)),
        name="moe_down",
    )(plan["texp"], plan["first"], plan["nvalid"], plan["nxt"], hmid, w_down)


def _pos2d(length, d):
    rows = length // GRID_W
    r = jnp.repeat(jnp.arange(rows, dtype=F32), GRID_W)
    col = jnp.tile(jnp.arange(GRID_W, dtype=F32), rows)
    nf = d // 4
    omega = 1.0 / (10000.0 ** (jnp.arange(nf, dtype=F32) / nf))

    def enc(p):
        a = p[:, None] * omega[None, :]
        return jnp.concatenate([jnp.sin(a), jnp.cos(a)], axis=-1)

    return jnp.concatenate([enc(r), enc(col)], axis=-1)


def kernel(x_prompt, x_sample, state_ssm_re, state_ssm_im, c, c_ctx, w_ada, b_ada, ln_g, ln_b, s5_w_in, s5_lam_re, s5_lam_im, s5_log_dt, s5_b_re, s5_b_im, s5_c_re, s5_c_im, s5_d, s5_w_glu, s5_w_out, g_w_in, g_b_in, g_ln_g, g_ln_b, g_w_s, g_b_s, g_w_out, w_router, b_router, moe_w_gate, moe_w_up, moe_w_down):
    bp, seq, d = x_prompt.shape
    bs, ls, _ = x_sample.shape
    depth = w_ada.shape[0]
    n_exp = w_router.shape[1]
    tp, ts = bp * seq, bs * ls
    alpha = float((2 * depth) ** 0.25)
    tm = min(256, seq)
    tme = min(512, tp)
    pitch = _pitch(d // LANES)
    assert bs + 1 <= COND_ROWS and d % LANES == 0 and seq % tm == 0 and ls % tm == 0

    cond = jnp.concatenate([c_ctx[None].astype(F32), c.astype(F32),
                            jnp.zeros((COND_ROWS - 1 - bs, d), F32)], axis=0)
    mods = _adaln(cond, w_ada, b_ada)
    x, h = _embed(x_prompt.reshape(tp, d), x_sample.reshape(ts, d), _pos2d(ls, d), mods, tm, ls)
    wr, br = w_router.T, b_router.reshape(n_exp, 1)
    g_b_s_t = jnp.swapaxes(g_b_s, -1, -2)

    new_re, new_im = [], []
    for i in range(depth):
        j = i // 2
        if i % 2 == 0:
            wall, qt, a8 = _s5_prep(s5_lam_re[j], s5_lam_im[j], s5_log_dt[j], s5_b_re[j], s5_b_im[j],
                                    s5_c_re[j], s5_c_im[j])
            g, n = s5_lam_re.shape[2:]
            nsp = ts // min(4096, tp)
            h0 = jnp.stack([state_ssm_re[:, j, 0], state_ssm_im[:, j, 0], state_ssm_re[:, j, 1],
                            state_ssm_im[:, j, 1]]).reshape(4, nsp, bs // nsp, g * n).swapaxes(0, 1)
            u = _matmul(h, s5_w_in, (j,), name="s5_in")
            z, fin = _s5_scan(u, s5_d, j, wall, qt, a8, h0, tp, seq, ls, s5_b_re.shape[-1])
            fin = fin.swapaxes(0, 1).reshape(4, bp, g, n)
            new_re.append(jnp.stack([fin[0], fin[2]], axis=1))
            new_im.append(jnp.stack([fin[1], fin[3]], axis=1))
            tmm, tnn = min(MM_TM, tp + ts), min(MM_TN, d)
            zg = _matmul(z, s5_w_glu, (j,), epilogue=_epi_glu, extra=(z,),
                         extra_specs=(pl.BlockSpec((tmm, tnn), lambda b, a: (a, b)),),
                         out_dtype=BF16, name="s5_glu")
            ymix = _matmul(zg, s5_w_out, (j,), out_dtype=BF16, name="s5_out")
        else:
            tnn = min(MM_TN, 2 * d)
            zz = _matmul(h, g_w_in, (j,), epilogue=_epi_bias_gelu,
                         extra=(g_b_in.reshape(-1, 1, 2 * d),),
                         extra_specs=(pl.BlockSpec((None, 1, tnn), lambda b, a: (j, 0, b)),),
                         out_dtype=BF16, name="gmlp_in")
            gated = _gmlp_gate(zz, g_ln_g, g_ln_b, g_w_s, g_b_s_t, j)
            ymix = _matmul(gated, g_w_out, (j,), out_dtype=BF16, name="gmlp_out")
        x, hp, route = _postnorm_a(x, ymix, mods, i, ln_g, ln_b, wr, br, tm, tp, ls, alpha, pitch)
        plan = _plan(route, tme, n_exp)
        xs = _scatter_rows(hp.reshape(tp + ts, pitch, LANES), plan, tme, n_exp, tm)
        hmid = _moe_a(xs.reshape(-1, LANES), plan, moe_w_gate, moe_w_up, i, tme, pitch)
        ysp = _moe_b(hmid, plan, moe_w_down, i, tme, pitch)
        x, h = _postnorm_b(x, ysp.reshape(-1, pitch, LANES), plan["pos"], plan["cw"], mods, i, ln_g, ln_b,
                           tm, tp, ls, alpha, pitch, last=(i == depth - 1))
    return (x.reshape(bp, seq, d), h.reshape(bs, ls, d),
            jnp.stack(new_re, axis=1).astype(x_prompt.dtype), jnp.stack(new_im, axis=1).astype(x_prompt.dtype))
```

```python
import functools

import jax
import jax.numpy as jnp
from jax import lax
from jax.experimental import pallas as pl
from jax.experimental.pallas import tpu as pltpu

F32, BF16, U32, I32 = jnp.float32, jnp.bfloat16, jnp.uint32, jnp.int32

LN_EPS = 1e-5
GRID_W = 64
N_MOD = 6
N_EXPERT_GROUPS = 4
COND_ROWS = 16
LANES = 128
SUBLANES = 8
CHUNK = SUBLANES
PAIR = 2 * LANES
V7X_VMEM_LIMIT = 56 * 2**20
HIGHEST = lax.Precision.HIGHEST


def _params(sem, vmem=V7X_VMEM_LIMIT):
    return pltpu.CompilerParams(dimension_semantics=sem, vmem_limit_bytes=vmem)


def _cond_row(tile, tm, tp, ls):
    start = tile * tm
    return jnp.where(start < tp, 0, 1 + jnp.maximum(start - tp, 0) // ls)


def _ln(x, g, b):
    mu = jnp.mean(x, axis=-1, keepdims=True)
    xc = x - mu
    var = jnp.mean(xc * xc, axis=-1, keepdims=True)
    return xc * lax.rsqrt(var + LN_EPS) * g + b


def _pitch(nchunks):
    p = -(-nchunks // SUBLANES) * SUBLANES
    if (p // SUBLANES) % 2 == 0:
        p += SUBLANES
    return p


def _pack_pair(a, b):
    ua = lax.bitcast_convert_type(a.astype(BF16).astype(F32), U32)
    ub = lax.bitcast_convert_type(b.astype(BF16).astype(F32), U32)
    return (ua >> 16) | ub


def _unpack_pair(w):
    lo = lax.bitcast_convert_type(w << 16, F32)
    hi = lax.bitcast_convert_type(w & jnp.uint32(0xFFFF0000), F32)
    return lo, hi


def _store_token_major(ref, h, rows, pitch):
    nch = h.shape[1] // PAIR
    for s in range(nch):
        ref[pl.ds(s, rows, stride=pitch), :] = _pack_pair(
            h[:, s * PAIR:s * PAIR + LANES], h[:, s * PAIR + LANES:(s + 1) * PAIR])
    for s in range(nch, pitch):
        ref[pl.ds(s, rows, stride=pitch), :] = jnp.zeros((rows, LANES), U32)


def _adaln_kernel(c_ref, w_ref, b_ref, o_ref):
    s = jax.nn.silu(c_ref[...]).astype(BF16)
    o_ref[...] = jnp.dot(s, w_ref[...].astype(BF16), preferred_element_type=F32) + b_ref[...]


def _adaln(cond, w_ada, b_ada):
    depth, d, nm = w_ada.shape
    tn = min(512, nm)
    return pl.pallas_call(
        _adaln_kernel,
        out_shape=jax.ShapeDtypeStruct((depth, COND_ROWS, nm), F32),
        grid=(depth, nm // tn),
        in_specs=[pl.BlockSpec((COND_ROWS, d), lambda i, n: (0, 0)),
                  pl.BlockSpec((None, d, tn), lambda i, n: (i, 0, n)),
                  pl.BlockSpec((None, 1, tn), lambda i, n: (i, 0, n))],
        out_specs=pl.BlockSpec((None, COND_ROWS, tn), lambda i, n: (i, 0, n)),
        compiler_params=_params(("arbitrary", "arbitrary")),
        name="adaln",
    )(cond, w_ada, b_ada.reshape(depth, 1, nm))


def _mod_spec(d, layer, m):
    return pl.BlockSpec((None, COND_ROWS, d), lambda *_: (layer, 0, m))


def _embed_kernel(xp_ref, xs_ref, pos_ref, sh_ref, sc_ref, x_ref, h_ref, *, tm, tp, ls):
    i = pl.program_id(0)
    r = _cond_row(i, tm, tp, ls)
    sh = sh_ref[pl.ds(r, 1), :]
    sc = sc_ref[pl.ds(r, 1), :]

    def emit(x):
        x_ref[...] = x
        h_ref[...] = (x * (1 + sc) + sh).astype(BF16)

    @pl.when(i * tm < tp)
    def _():
        emit(xp_ref[...])

    @pl.when(i * tm >= tp)
    def _():
        emit(xs_ref[...] + pos_ref[...])


def _embed(xp, xs, pos, mods, tm, ls):
    tp, d = xp.shape
    ts = xs.shape[0]
    npt, t = tp // tm, tp + ts
    nps = ls // tm
    return pl.pallas_call(
        functools.partial(_embed_kernel, tm=tm, tp=tp, ls=ls),
        out_shape=(jax.ShapeDtypeStruct((t, d), F32), jax.ShapeDtypeStruct((t, d), BF16)),
        grid=(t // tm,),
        in_specs=[pl.BlockSpec((tm, d), lambda i: (jnp.minimum(i, npt - 1), 0)),
                  pl.BlockSpec((tm, d), lambda i: (jnp.maximum(i - npt, 0), 0)),
                  pl.BlockSpec((tm, d), lambda i: (jnp.maximum(i - npt, 0) % nps, 0)),
                  _mod_spec(d, 0, 0), _mod_spec(d, 0, 1)],
        out_specs=(pl.BlockSpec((tm, d), lambda i: (i, 0)), pl.BlockSpec((tm, d), lambda i: (i, 0))),
        compiler_params=_params(("arbitrary",)),
        name="embed",
    )(xp, xs, pos, mods, mods)


def _epi_store(acc, extra, outs):
    outs[0][...] = acc.astype(outs[0].dtype)


def _epi_bias_gelu(acc, extra, outs):
    outs[0][...] = jax.nn.gelu(acc + extra[0][...]).astype(outs[0].dtype)


def _epi_glu(acc, extra, outs):
    outs[0][...] = (extra[0][...].astype(F32) * jax.nn.sigmoid(acc)).astype(outs[0].dtype)


MM_TM, MM_TN = 512, 1024


def _mm_kernel(*refs, n_extra, epilogue):
    x_ref, w_ref = refs[:2]
    extra, outs, w_s = refs[2:2 + n_extra], refs[2 + n_extra:-1], refs[-1]

    @pl.when(pl.program_id(1) == 0)
    def _():
        w_s[...] = w_ref[...].astype(BF16)

    epilogue(jnp.dot(x_ref[...], w_s[...], preferred_element_type=F32), extra, outs)


def _matmul(x, w, w_prefix, *, epilogue=_epi_store, extra=(), extra_specs=(), out_dtype=F32, name):
    m, kdim = x.shape
    n = w.shape[-1]
    tm, tn = min(MM_TM, m), min(MM_TN, n)
    npre = len(w_prefix)
    return pl.pallas_call(
        functools.partial(_mm_kernel, n_extra=len(extra), epilogue=epilogue),
        out_shape=jax.ShapeDtypeStruct((m, n), out_dtype),
        grid=(n // tn, m // tm),
        in_specs=[pl.BlockSpec((tm, kdim), lambda j, i: (i, 0)),
                  pl.BlockSpec((None,) * npre + (kdim, tn), lambda j, i: tuple(w_prefix) + (0, j),
                               pipeline_mode=pl.Buffered(1)),
                  *extra_specs],
        out_specs=pl.BlockSpec((tm, tn), lambda j, i: (i, j)),
        scratch_shapes=[pltpu.VMEM((kdim, tn), BF16)],
        compiler_params=_params(("arbitrary", "arbitrary")),
        name=name,
    )(x, w, *extra)


def _s5_prep_kernel(lre_ref, lim_ref, ldt_ref, btr_ref, bti_ref, ctr_ref, cti_ref,
                    wp_ref, qt_ref, a8_ref, *, group_ch, state_n):
    half = lre_ref.shape[-1]
    gps = LANES // group_ch
    mask = (lax.broadcasted_iota(I32, (LANES, half), 0) // group_ch
            == lax.broadcasted_iota(I32, (LANES, half), 1) // state_n)
    wp_ref[...] = jnp.zeros_like(wp_ref)
    tstacks, a8rows = [], []
    for k in range(2):
        lr, li = lre_ref[k], lim_ref[k]
        dt = jnp.exp(ldt_ref[k])
        mag = jnp.exp(lr * dt)
        ar, ai = mag * jnp.cos(li * dt), mag * jnp.sin(li * dt)
        den = lr * lr + li * li
        cr = ((ar - 1) * lr + ai * li) / den
        ci = (ai * lr - (ar - 1) * li) / den
        pw = [(jnp.ones_like(ar), jnp.zeros_like(ar))]
        for _ in range(CHUNK):
            pr, pi = pw[-1]
            pw.append((pr * ar - pi * ai, pr * ai + pi * ar))
        br, bi = btr_ref[k], bti_ref[k]
        crr, cii = ctr_ref[k], cti_ref[k]

        def compact(ref, tiles, row_step, col0):
            for g in range(gps):
                q, gl = g // 2, g % 2
                rows = slice(gl * LANES + row_step * group_ch, gl * LANES + (row_step + 1) * group_ch)
                for j, tile in enumerate(tiles):
                    ref[q, rows, col0 + j * LANES:col0 + (j + 1) * LANES] = (
                        tile[g * group_ch:(g + 1) * group_ch, q * LANES:(q + 1) * LANES].astype(BF16))

        def own(tile, g):
            return tile[g * group_ch:(g + 1) * group_ch, (g // 2) * LANES:(g // 2 + 1) * LANES]

        ptiles = []
        for s in range(CHUNK):
            e = CHUNK - 1 - s if k == 0 else s
            zr = cr * pw[e][0] - ci * pw[e][1]
            zi = cr * pw[e][1] + ci * pw[e][0]
            pre, pim = jnp.where(mask, zr * br - zi * bi, 0.0), jnp.where(mask, zr * bi + zi * br, 0.0)
            ptiles.append((pre, pim))
            compact(wp_ref, (pre, pim), s, 2 * LANES * (1 + k))
        c0 = (jnp.where(mask, crr, 0.0), jnp.where(mask, -cii, 0.0))
        per_pair = []
        for q in range(gps // 2):
            p_pair = jnp.concatenate([jnp.concatenate([own(pre, 2 * q + gl), own(pim, 2 * q + gl)], axis=1)
                                      for gl in range(2) for pre, pim in ptiles], axis=0)
            c_pair = jnp.concatenate([jnp.concatenate([own(c0[0], 2 * q + gl), own(c0[1], 2 * q + gl)], axis=1)
                                      for gl in range(2)] + [jnp.zeros((LANES - 2 * group_ch, 2 * LANES), F32)],
                                     axis=0)
            per_pair.append(lax.dot_general(p_pair, c_pair, (((1,), (1,)), ((), ())),
                                            precision=HIGHEST, preferred_element_type=F32))
        tstacks.append(per_pair)
        for t in range(CHUNK):
            e = t + 1 if k == 0 else CHUNK - t
            compact(qt_ref, (jnp.where(mask, pw[e][0] * crr - pw[e][1] * cii, 0.0),
                             jnp.where(mask, -(pw[e][0] * cii + pw[e][1] * crr), 0.0)), t, 2 * LANES * k)
        a8rows += [pw[CHUNK][0], pw[CHUNK][1]]
    tf, tb = tstacks
    blk = group_ch
    for g in range(gps):
        q, gl = g // 2, g % 2
        tfg, tbg = tf[q][gl * LANES:(gl + 1) * LANES], tb[q][gl * LANES:(gl + 1) * LANES]
        m = jnp.zeros((LANES, LANES), F32)
        for t in range(CHUNK):
            shift = ((t - gl) * blk) % LANES
            rf = pltpu.roll(tfg, shift, axis=1) if shift else tfg
            rb = pltpu.roll(tbg, shift, axis=1) if shift else tbg
            up = (CHUNK - 1 - t) * blk
            m = m + (jnp.concatenate([rf[up:], jnp.zeros((up, LANES), F32)], axis=0) if up else rf)
            dn = t * blk
            m = m + (jnp.concatenate([jnp.zeros((dn, LANES), F32), rb[:LANES - dn]], axis=0) if dn else rb)
        wp_ref[q, gl * LANES:(gl + 1) * LANES, gl * LANES:(gl + 1) * LANES] = m.astype(BF16)
    a8_ref[...] = jnp.concatenate(a8rows + [jnp.zeros_like(a8rows[0])] * (SUBLANES - 4), axis=0)


def _s5_prep(lam_re, lam_im, log_dt, b_re, b_im, c_re, c_im):
    _, g, n = lam_re.shape
    gc = b_re.shape[-1]
    gps = LANES // gc
    nslab, half = g // gps, gps * n
    assert gps % 2 == 0 and 2 * n == LANES and gps == CHUNK

    def lane(a):
        return a.reshape(2, nslab, 1, half)

    def rows(a):
        return jnp.tile(a.reshape(2, nslab, LANES, n), (1, 1, 1, gps))

    ins = (lane(lam_re), lane(lam_im), lane(jnp.broadcast_to(log_dt[..., None], lam_re.shape)),
           rows(jnp.swapaxes(b_re, -1, -2)), rows(jnp.swapaxes(b_im, -1, -2)), rows(c_re), rows(c_im))
    lspec = pl.BlockSpec((2, None, 1, half), lambda c: (0, c, 0, 0))
    rspec = pl.BlockSpec((2, None, LANES, half), lambda c: (0, c, 0, 0))
    npair = gps // 2
    return pl.pallas_call(
        functools.partial(_s5_prep_kernel, group_ch=gc, state_n=n),
        out_shape=(jax.ShapeDtypeStruct((nslab, npair, 2 * LANES, 6 * LANES), BF16),
                   jax.ShapeDtypeStruct((nslab, npair, 2 * LANES, 4 * LANES), BF16),
                   jax.ShapeDtypeStruct((nslab, SUBLANES, half), F32)),
        grid=(nslab,),
        in_specs=[lspec, lspec, lspec, rspec, rspec, rspec, rspec],
        out_specs=(pl.BlockSpec((None, npair, 2 * LANES, 6 * LANES), lambda c: (c, 0, 0, 0)),
                   pl.BlockSpec((None, npair, 2 * LANES, 4 * LANES), lambda c: (c, 0, 0, 0)),
                   pl.BlockSpec((None, SUBLANES, half), lambda c: (c, 0, 0))),
        compiler_params=_params(("arbitrary",)),
        name="s5_prep",
    )(*ins)


def _lane_block_transpose(arrs, blk):
    n = len(arrs)
    assert n * blk == LANES
    lane_blk = lax.broadcasted_iota(I32, arrs[0].shape, 1) // blk
    cur, k = list(arrs), n // 2
    while k:
        keep = (lane_blk & k) == 0
        nxt = list(cur)
        for i in range(n):
            if i & k == 0:
                nxt[i] = jnp.where(keep, cur[i], pltpu.roll(cur[i + k], k * blk, axis=1))
                nxt[i + k] = jnp.where(keep, pltpu.roll(cur[i], LANES - k * blk, axis=1), cur[i + k])
        cur, k = nxt, k // 2
    return cur


def _s5_kernel(u_ref, d_ref, wp_ref, qt_ref, a8_ref, h0_ref, z_ref, fin_ref,
               xf_s, xb_s, sf_s, sb_s, zs_s, *, r, npp, nbp, mp, nbs, ms, group_ch):
    p = pl.program_id(1)
    rows = r // CHUNK
    half = a8_ref.shape[-1]
    nq = half // LANES
    us = [u_ref[pl.ds(s, rows, stride=CHUNK), :] for s in range(CHUNK)]
    vs = _lane_block_transpose(us, group_ch)
    bigs = [jnp.dot(jnp.concatenate([vs[2 * q], vs[2 * q + 1]], axis=1).astype(BF16), wp_ref[q],
                    preferred_element_type=F32) for q in range(nq)]
    a8 = a8_ref[...]
    afr, afi, abr, abi = a8[0:1], a8[1:2], a8[2:3], a8[3:4]
    d = d_ref[...]

    def split(a):
        return [a[:, q * LANES:(q + 1) * LANES] for q in range(a.shape[1] // LANES)]

    def run(nb, m, init):
        pitch = m + SUBLANES
        for b in range(nb):
            rb = slice(b * m, (b + 1) * m)
            for q in range(nq):
                for h in range(2):
                    xf_s[h * nq + q, b * pitch:b * pitch + m, :] = bigs[q][rb, (2 + h) * LANES:(3 + h) * LANES]
                    xb_s[h * nq + q, b * pitch:b * pitch + m, :] = bigs[q][rb, (4 + h) * LANES:(5 + h) * LANES]

        def step(j, st, ar, ai, x_s, s_s):
            new_r, new_i = [], []
            for q in range(nq):
                sr, si = st[q], st[nq + q]
                s_s[q, pl.ds(j, nb, stride=pitch), :] = sr
                s_s[nq + q, pl.ds(j, nb, stride=pitch), :] = si
                xr = x_s[q, pl.ds(j, nb, stride=pitch), :]
                xi = x_s[nq + q, pl.ds(j, nb, stride=pitch), :]
                new_r.append(ar[q] * sr - ai[q] * si + xr)
                new_i.append(ar[q] * si + ai[q] * sr + xi)
            return tuple(new_r + new_i)

        fr, fi, br_, bi_ = split(afr), split(afi), split(abr), split(abi)
        def both(j, st):
            return (step(j, st[0], fr, fi, xf_s, sf_s), step(m - 1 - j, st[1], br_, bi_, xb_s, sb_s))

        fin_f, fin_b = lax.fori_loop(0, m, both, (tuple(split(init[0]) + split(init[1])),
                                                   tuple(split(init[2]) + split(init[3]))), unroll=2)
        yg = []
        for q in range(nq):
            sq = jnp.concatenate(
                [jnp.concatenate([s_s[h * nq + q, b * pitch:b * pitch + m, :] for s_s in (sf_s, sb_s)
                                  for h in range(2)], axis=1) for b in range(nb)], axis=0).astype(BF16)
            yi = lax.dot_general(sq, qt_ref[q], (((1,), (1,)), ((), ())), preferred_element_type=F32)
            yg += [bigs[q][:, :LANES] + yi[:, :LANES], bigs[q][:, LANES:2 * LANES] + yi[:, LANES:]]
        ys = _lane_block_transpose(yg, group_ch)
        for t in range(CHUNK):
            zs_s[pl.ds(t, rows, stride=CHUNK), :] = jax.nn.gelu(ys[t] + d * us[t])
        z_ref[...] = zs_s[...].astype(BF16)
        return [jnp.concatenate(f[h * nq:(h + 1) * nq], axis=1) for f in (fin_f, fin_b) for h in range(2)]

    @pl.when(p < npp)
    def _():
        zero = jnp.zeros((nbp, half), F32)
        fin = run(nbp, mp, (zero, zero, zero, zero))
        for q in range(4):
            fin_ref[q] = fin[q]

    @pl.when(p >= npp)
    def _():
        run(nbs, ms, tuple(h0_ref[q] for q in range(4)))


def _s5_scan(u, d_skip, layer, wp, qt, a8, h0, tp, seq, ls, group_ch):
    t, dm = u.shape
    nslab, half = a8.shape[0], a8.shape[-1]
    r = min(4096, tp)
    assert tp % r == 0 and (t - tp) % r == 0 and r % seq == 0 and r % ls == 0
    npp, nparts = tp // r, t // r
    nbp, mp, nbs, ms = r // seq, seq // CHUNK, r // ls, ls // CHUNK
    nsp = nparts - npp
    srows = max(nbp * (mp + SUBLANES), nbs * (ms + SUBLANES))
    z, fin = pl.pallas_call(
        functools.partial(_s5_kernel, r=r, npp=npp, nbp=nbp, mp=mp, nbs=nbs, ms=ms, group_ch=group_ch),
        out_shape=(jax.ShapeDtypeStruct((t, dm), BF16),
                   jax.ShapeDtypeStruct((npp, 4, nbp, nslab * half), F32)),
        grid=(nslab, nparts),
        in_specs=[pl.BlockSpec((r, LANES), lambda c, p: (p, c)),
                  pl.BlockSpec((None, 1, LANES), lambda c, p: (layer, 0, c)),
                  pl.BlockSpec((None,) + wp.shape[1:], lambda c, p: (c, 0, 0, 0)),
                  pl.BlockSpec((None,) + qt.shape[1:], lambda c, p: (c, 0, 0, 0)),
                  pl.BlockSpec((None, SUBLANES, half), lambda c, p: (c, 0, 0)),
                  pl.BlockSpec((None, 4, nbs, half), lambda c, p: (jnp.clip(p - npp, 0, nsp - 1), 0, 0, c))],
        out_specs=(pl.BlockSpec((r, LANES), lambda c, p: (p, c)),
                   pl.BlockSpec((None, 4, nbp, half), lambda c, p: (jnp.minimum(p, npp - 1), 0, 0, c))),
        scratch_shapes=[pltpu.VMEM((2 * half // LANES, srows, LANES), F32)] * 4 + [pltpu.VMEM((r, LANES), F32)],
        compiler_params=_params(("arbitrary", "arbitrary")),
        name="s5_scan",
    )(u, d_skip.reshape(d_skip.shape[0], 1, dm), wp, qt, a8, h0)
    return z, fin


def _gmlp_kernel(u_ref, v_ref, g_ref, b_ref, ws_ref, bs_ref, o_ref, *, heads, hd):
    v = _ln(v_ref[...].astype(F32), g_ref[...], b_ref[...]).astype(BF16)
    for h in range(heads):
        sl = slice(h * hd, (h + 1) * hd)
        s = jnp.dot(ws_ref[h].astype(BF16), v[:, sl], preferred_element_type=F32) + bs_ref[:, h:h + 1]
        o_ref[:, sl] = (u_ref[:, sl].astype(F32) * s).astype(BF16)


def _gmlp_gate(z, g_ln_g, g_ln_b, w_s, b_s_t, layer):
    t, d2 = z.shape
    d = d2 // 2
    _, heads, ck, _ = w_s.shape
    return pl.pallas_call(
        functools.partial(_gmlp_kernel, heads=heads, hd=d // heads),
        out_shape=jax.ShapeDtypeStruct((t, d), BF16),
        grid=(t // ck,),
        in_specs=[pl.BlockSpec((ck, d), lambda i: (i, 0)),
                  pl.BlockSpec((ck, d), lambda i: (i, 1)),
                  pl.BlockSpec((None, 1, d), lambda i: (layer, 0, 0)),
                  pl.BlockSpec((None, 1, d), lambda i: (layer, 0, 0)),
                  pl.BlockSpec((None, heads, ck, ck), lambda i: (layer, 0, 0, 0)),
                  pl.BlockSpec((None, ck, heads), lambda i: (layer, 0, 0))],
        out_specs=pl.BlockSpec((ck, d), lambda i: (i, 0)),
        compiler_params=_params(("arbitrary",)),
        name="gmlp_gate",
    )(z, z, g_ln_g.reshape(-1, 1, d), g_ln_b.reshape(-1, 1, d), w_s, b_s_t)


def _second_of4(a, b, c, d):
    return jnp.maximum(jnp.maximum(jnp.minimum(a, b), jnp.minimum(c, d)),
                       jnp.minimum(jnp.maximum(a, b), jnp.maximum(c, d)))


def _first_argmax(vals):
    idx = jnp.zeros(vals[0].shape, I32)
    best = vals[0]
    for k in range(1, len(vals)):
        upd = vals[k] > best
        idx = jnp.where(upd, k, idx)
        best = jnp.where(upd, vals[k], best)
    return idx


def _pick(idx, vals):
    out = vals[0]
    for k in range(1, len(vals)):
        out = jnp.where(idx == k, vals[k], out)
    return out


def _route(h, wr, br):
    n_exp = wr.shape[0]
    epg = n_exp // N_EXPERT_GROUPS
    logits = lax.dot_general(wr, h, (((1,), (1,)), ((), ())), precision=HIGHEST, preferred_element_type=F32)
    ex = jnp.exp(logits - jnp.max(logits, axis=0, keepdims=True))
    probs = ex / jnp.sum(ex, axis=0, keepdims=True)
    sel = probs + br
    srow = [sel[e:e + 1] for e in range(n_exp)]
    prow = [probs[e:e + 1] for e in range(n_exp)]
    scores = []
    for gi in range(N_EXPERT_GROUPS):
        a, b, c, d = srow[gi * epg:(gi + 1) * epg]
        scores.append(jnp.maximum(jnp.maximum(a, b), jnp.maximum(c, d)) + _second_of4(a, b, c, d))
    bg = _first_argmax(scores)
    vs = [_pick(bg, [srow[gi * epg + k] for gi in range(N_EXPERT_GROUPS)]) for k in range(epg)]
    ps = [_pick(bg, [prow[gi * epg + k] for gi in range(N_EXPERT_GROUPS)]) for k in range(epg)]
    i1 = _first_argmax(vs)
    i2 = _first_argmax([jnp.where(i1 == k, -jnp.inf, vs[k]) for k in range(epg)])
    g1, g2 = _pick(i1, ps), _pick(i2, ps)
    tot = g1 + g2
    zero = jnp.zeros_like(g1)
    return jnp.concatenate([(bg * epg + i1).astype(F32), (bg * epg + i2).astype(F32), g1 / tot, g2 / tot,
                            zero, zero, zero, zero], axis=0)


def _postnorm_a_kernel(x_ref, y_ref, gate_ref, g_ref, b_ref, sh_ref, sc_ref, wr_ref, br_ref,
                       x1_ref, hp_ref, route_ref, *, tm, tp, ls, alpha, pitch):
    r = _cond_row(pl.program_id(0), tm, tp, ls)
    v = alpha * x_ref[...] + gate_ref[pl.ds(r, 1), :] * y_ref[...].astype(F32)
    x1 = _ln(v, g_ref[...], b_ref[...])
    x1_ref[...] = x1
    h = x1 * (1 + sc_ref[pl.ds(r, 1), :]) + sh_ref[pl.ds(r, 1), :]
    _store_token_major(hp_ref, h, tm, pitch)
    route_ref[...] = _route(h, wr_ref[...], br_ref[...])


def _postnorm_a(x, y, mods, layer, ln_g, ln_b, wr, br, tm, tp, ls, alpha, pitch):
    t, d = x.shape
    n_exp = wr.shape[0]
    row = lambda i: (i, 0)
    lnspec = pl.BlockSpec((None, None, 1, d), lambda i: (layer, 0, 0, 0))
    return pl.pallas_call(
        functools.partial(_postnorm_a_kernel, tm=tm, tp=tp, ls=ls, alpha=alpha, pitch=pitch),
        out_shape=(jax.ShapeDtypeStruct((t, d), F32), jax.ShapeDtypeStruct((t * pitch, LANES), U32),
                   jax.ShapeDtypeStruct((SUBLANES, t), F32)),
        grid=(t // tm,),
        in_specs=[pl.BlockSpec((tm, d), row), pl.BlockSpec((tm, d), row), _mod_spec(d, layer, 2),
                  lnspec, lnspec, _mod_spec(d, layer, 3), _mod_spec(d, layer, 4),
                  pl.BlockSpec((n_exp, d), lambda i: (0, 0)), pl.BlockSpec((n_exp, 1), lambda i: (0, 0))],
        out_specs=(pl.BlockSpec((tm, d), row), pl.BlockSpec((tm * pitch, LANES), row),
                   pl.BlockSpec((SUBLANES, tm), lambda i: (0, i))),
        compiler_params=_params(("arbitrary",)),
        name="postnorm_a",
    )(x, y, mods, ln_g.reshape(ln_g.shape[0], 2, 1, d), ln_b.reshape(ln_b.shape[0], 2, 1, d), mods, mods, wr, br)


def _postnorm_b_kernel(*refs, tm, tp, ls, alpha, pitch, nch, last):
    (pos_cur, pos_nxt, x_ref, ys_ref, ys2_ref, cw_ref, gate_ref, g_ref, b_ref) = refs[:9]
    buf, sem = refs[-2:]
    i, n = pl.program_id(0), pl.num_programs(0)
    slot = i % 2
    r = _cond_row(i, tm, tp, ls)

    def issue(pos_ref, sl):
        def body(tk, c):
            for k in range(2):
                pltpu.make_async_copy(ys_ref.at[pos_ref[0, 2 * tk + k]],
                                      buf.at[sl, pl.ds(pl.multiple_of((k * tm + tk) * pitch, SUBLANES), pitch)],
                                      sem.at[sl]).start(priority=k)
            return c

        lax.fori_loop(0, tm, body, 0, unroll=4)

    @pl.when(i == 0)
    def _():
        issue(pos_cur, 0)

    @pl.when(i + 1 < n)
    def _():
        issue(pos_nxt, 1 - slot)

    pltpu.make_async_copy(ys2_ref.at[pl.ds(0, 2 * tm * pitch)], buf.at[slot], sem.at[slot]).wait()

    c0, c1 = cw_ref[:, 0:1], cw_ref[:, 1:2]
    pieces = []
    for s in range(nch):
        lo0, hi0 = _unpack_pair(buf[slot, pl.ds(s, tm, stride=pitch), :])
        lo1, hi1 = _unpack_pair(buf[slot, pl.ds(tm * pitch + s, tm, stride=pitch), :])
        pieces += [c0 * lo0 + c1 * lo1, c0 * hi0 + c1 * hi1]
    f = jnp.concatenate(pieces, axis=1)
    x2 = _ln(alpha * x_ref[...] + gate_ref[pl.ds(r, 1), :] * f, g_ref[...], b_ref[...])
    if last:
        yp_ref, yl_ref = refs[9:11]

        @pl.when(i * tm < tp)
        def _():
            yp_ref[...] = x2

        @pl.when(i * tm >= tp)
        def _():
            yl_ref[...] = x2
    else:
        sh_ref, sc_ref, x2_ref, h_ref = refs[9:13]
        x2_ref[...] = x2
        h_ref[...] = (x2 * (1 + sc_ref[pl.ds(r, 1), :]) + sh_ref[pl.ds(r, 1), :]).astype(BF16)


def _postnorm_b(x, ys3, pos, cw, mods, layer, ln_g, ln_b, tm, tp, ls, alpha, pitch, last):
    t, d = x.shape
    n = t // tm
    row = lambda i: (i, 0)
    lnspec = pl.BlockSpec((None, None, 1, d), lambda i: (layer, 1, 0, 0))
    pos3 = pos.reshape(n, 1, 2 * tm)
    in_specs = [pl.BlockSpec((None, 1, 2 * tm), lambda i: (i, 0, 0), memory_space=pltpu.SMEM),
                pl.BlockSpec((None, 1, 2 * tm), lambda i: (jnp.minimum(i + 1, n - 1), 0, 0), memory_space=pltpu.SMEM),
                pl.BlockSpec((tm, d), row), pl.BlockSpec(memory_space=pl.ANY), pl.BlockSpec(memory_space=pl.ANY),
                pl.BlockSpec((tm, 2), row), _mod_spec(d, layer, 5), lnspec, lnspec]
    args = [pos3, pos3, x, ys3, ys3.reshape(-1, LANES), cw, mods, ln_g.reshape(ln_g.shape[0], 2, 1, d), ln_b.reshape(ln_b.shape[0], 2, 1, d)]
    if last:
        npt = tp // tm
        out_shape = (jax.ShapeDtypeStruct((tp, d), F32), jax.ShapeDtypeStruct((t - tp, d), F32))
        out_specs = (pl.BlockSpec((tm, d), lambda i: (jnp.minimum(i, npt - 1), 0)),
                     pl.BlockSpec((tm, d), lambda i: (jnp.maximum(i - npt, 0), 0)))
    else:
        in_specs += [_mod_spec(d, layer + 1, 0), _mod_spec(d, layer + 1, 1)]
        args += [mods, mods]
        out_shape = (jax.ShapeDtypeStruct((t, d), F32), jax.ShapeDtypeStruct((t, d), BF16))
        out_specs = (pl.BlockSpec((tm, d), row), pl.BlockSpec((tm, d), row))
    return pl.pallas_call(
        functools.partial(_postnorm_b_kernel, tm=tm, tp=tp, ls=ls, alpha=alpha, pitch=pitch,
                          nch=d // PAIR, last=last),
        out_shape=out_shape, grid=(n,), in_specs=in_specs, out_specs=out_specs,
        scratch_shapes=[pltpu.VMEM((2, 2 * tm * pitch, LANES), U32), pltpu.SemaphoreType.DMA((2,))],
        compiler_params=_params(("arbitrary",)),
        name="postnorm_b",
    )(*args)


def _plan(route, tme, n_exp):
    t = route.shape[1]
    ef = route[0:2].astype(I32).T.reshape(-1)
    oh = (ef[:, None] == jnp.arange(n_exp, dtype=I32)[None]).astype(I32)
    cs = jnp.cumsum(oh, axis=0)
    counts = cs[-1]
    ntile = (counts + tme - 1) // tme
    tend = jnp.cumsum(ntile)
    tstart = tend - ntile
    pos = jnp.sum(oh * (tstart[None] * tme + cs - 1), axis=1)
    nt = (2 * t) // tme + n_exp
    tid = jnp.arange(nt, dtype=I32)
    texp = jnp.minimum(jnp.sum((tid[:, None] >= tend[None]).astype(I32), axis=1), n_exp - 1)
    first = jnp.concatenate([jnp.ones((1,), I32), (texp[1:] != texp[:-1]).astype(I32)])
    after = tend[texp]
    nxt = jnp.where(after < tend[-1], texp[jnp.minimum(after, nt - 1)], -1)
    return dict(pos=pos, counts=counts, tstart=tstart, ntile=ntile, texp=texp, first=first, nxt=nxt,
                nvalid=tend[-1:], nt=nt, cw=route[2:4].T)


ZERO_ROWS = 64


def _scatter_kernel(cnt_ref, tst_ref, ntl_ref, nv_ref, pos_ref, src_ref, dst_ref, zero_s, sem, zsem,
                    *, tms, pitch, tme, n_exp, nt):
    i = pl.program_id(0)

    @pl.when(i == 0)
    def _():
        zero_s[...] = jnp.zeros_like(zero_s)
        per_tile = tme // ZERO_ROWS

        def tail(q, c):
            pltpu.make_async_copy(zero_s, dst_ref.at[pl.ds(q * ZERO_ROWS, ZERO_ROWS)], zsem).start()
            return c

        def tail_wait(_, c):
            pltpu.make_async_copy(zero_s, dst_ref.at[pl.ds(0, ZERO_ROWS)], zsem).wait()
            return c

        lax.fori_loop(nv_ref[0] * per_tile, nt * per_tile, tail, 0)
        lax.fori_loop(nv_ref[0] * per_tile, nt * per_tile, tail_wait, 0)

        def pad(rw, c):
            pltpu.make_async_copy(zero_s.at[0], dst_ref.at[rw], zsem).start()
            return c

        def pad_wait(_, c):
            pltpu.make_async_copy(zero_s.at[0], dst_ref.at[0], zsem).wait()
            return c

        for e in range(n_exp):
            lo = tst_ref[e] * tme + cnt_ref[e]
            hi = (tst_ref[e] + ntl_ref[e]) * tme
            lax.fori_loop(lo, hi, pad, 0)
            lax.fori_loop(lo, hi, pad_wait, 0)

    def move(tk, c):
        for k in range(2):
            pltpu.make_async_copy(src_ref.at[tk], dst_ref.at[pos_ref[0, 2 * tk + k]], sem).start(priority=k)
        return c

    lax.fori_loop(0, tms, move, 0, unroll=4)
    for _ in range(2):
        pltpu.make_async_copy(src_ref, dst_ref.at[pl.ds(0, tms)], sem).wait()


def _scatter_rows(hp3, plan, tme, n_exp, tms):
    t, pitch, _ = hp3.shape
    n = t // tms
    assert tme % ZERO_ROWS == 0
    return pl.pallas_call(
        functools.partial(_scatter_kernel, tms=tms, pitch=pitch, tme=tme, n_exp=n_exp, nt=plan["nt"]),
        out_shape=jax.ShapeDtypeStruct((plan["nt"] * tme, pitch, LANES), U32),
        grid_spec=pltpu.PrefetchScalarGridSpec(
            num_scalar_prefetch=4, grid=(n,),
            in_specs=[pl.BlockSpec((None, 1, 2 * tms), lambda i, *_: (i, 0, 0), memory_space=pltpu.SMEM),
                      pl.BlockSpec((tms, pitch, LANES), lambda i, *_: (i, 0, 0))],
            out_specs=pl.BlockSpec(memory_space=pl.ANY),
            scratch_shapes=[pltpu.VMEM((ZERO_ROWS, pitch, LANES), U32), pltpu.SemaphoreType.DMA(()),
                            pltpu.SemaphoreType.DMA(())]),
        compiler_params=pltpu.CompilerParams(dimension_semantics=("arbitrary",), has_side_effects=True,
                                             vmem_limit_bytes=V7X_VMEM_LIMIT),
        name="moe_scatter",
    )(plan["counts"], plan["tstart"], plan["ntile"], plan["nvalid"], plan["pos"].reshape(n, 1, 2 * tms), hp3)


def _expert_weights(i, first_ref, te_ref, nxt_ref, restart, copies, stages, caches):
    @pl.when(first_ref[i] == 1)
    def _():
        @pl.when(restart)
        def _():
            for cp in copies(te_ref[i]):
                cp.start()

        for cp in copies(te_ref[i]):
            cp.wait()
        for stage, cache in zip(stages, caches):
            cache[...] = stage[...].astype(BF16)

        @pl.when(nxt_ref[i] >= 0)
        def _():
            for cp in copies(nxt_ref[i]):
                cp.start()


def _moe_a_kernel(te_ref, first_ref, nv_ref, nxt_ref, xs_ref, wg_ref, wu_ref, h_ref,
                  lhs_s, wg_s, wu_s, sg_s, su_s, sem, *, layer, tme, tna, nch, pitch):
    fi, i = pl.program_id(0), pl.program_id(1)

    def copies(e):
        cols = pl.ds(pl.multiple_of(fi * tna, LANES), tna)
        return (pltpu.make_async_copy(wg_ref.at[layer, e, :, cols], sg_s, sem.at[0]),
                pltpu.make_async_copy(wu_ref.at[layer, e, :, cols], su_s, sem.at[1]))

    @pl.when(i < nv_ref[0])
    def _():
        _expert_weights(i, first_ref, te_ref, nxt_ref, i == 0, copies, (sg_s, su_s), (wg_s, wu_s))
        for s in range(nch):
            lo, hi = _unpack_pair(xs_ref[pl.ds(s, tme, stride=pitch), :])
            lhs_s[:, s * PAIR:s * PAIR + LANES] = lo.astype(BF16)
            lhs_s[:, s * PAIR + LANES:(s + 1) * PAIR] = hi.astype(BF16)
        x = lhs_s[...]
        g = jnp.dot(x, wg_s[...], preferred_element_type=F32)
        u = jnp.dot(x, wu_s[...], preferred_element_type=F32)
        h_ref[...] = (jax.nn.silu(g) * u).astype(BF16)

    @pl.when(i >= nv_ref[0])
    def _():
        h_ref[...] = jnp.zeros_like(h_ref)


def _moe_a(xs2, plan, w_gate, w_up, layer, tme, pitch):
    _, _, d, f = w_gate.shape
    nt = plan["nt"]
    tna = min(512, f)
    return pl.pallas_call(
        functools.partial(_moe_a_kernel, layer=layer, tme=tme, tna=tna, nch=d // PAIR, pitch=pitch),
        out_shape=jax.ShapeDtypeStruct((nt * tme, f), BF16),
        grid_spec=pltpu.PrefetchScalarGridSpec(
            num_scalar_prefetch=4, grid=(f // tna, nt),
            in_specs=[pl.BlockSpec((tme * pitch, LANES), lambda fi, i, te, fr, nv, nx: (jnp.minimum(i, nv[0] - 1), 0)),
                      pl.BlockSpec(memory_space=pl.ANY), pl.BlockSpec(memory_space=pl.ANY)],
            out_specs=pl.BlockSpec((tme, tna), lambda fi, i, te, fr, nv, nx: (i, fi)),
            scratch_shapes=[pltpu.VMEM((tme, d), BF16), pltpu.VMEM((d, tna), BF16), pltpu.VMEM((d, tna), BF16),
                            pltpu.VMEM((d, tna), F32), pltpu.VMEM((d, tna), F32), pltpu.SemaphoreType.DMA((2,))]),
        compiler_params=_params(("arbitrary", "arbitrary")),
        name="moe_up",
    )(plan["texp"], plan["first"], plan["nvalid"], plan["nxt"], xs2, w_gate, w_up)


def _moe_b_kernel(te_ref, first_ref, nv_ref, nxt_ref, h_ref, wd_ref, ys_ref, wd_s, sd_s, sem,
                  *, layer, tme, pitch):
    i = pl.program_id(0)

    def copies(e):
        return (pltpu.make_async_copy(wd_ref.at[layer, e], sd_s, sem.at[0]),)

    @pl.when(i < nv_ref[0])
    def _():
        _expert_weights(i, first_ref, te_ref, nxt_ref, i == 0, copies, (sd_s,), (wd_s,))
        _store_token_major(ys_ref, jnp.dot(h_ref[...], wd_s[...], preferred_element_type=F32), tme, pitch)

    @pl.when(i >= nv_ref[0])
    def _():
        ys_ref[...] = jnp.zeros_like(ys_ref)


def _moe_b(hmid, plan, w_down, layer, tme, pitch):
    _, _, f, d = w_down.shape
    nt = plan["nt"]
    return pl.pallas_call(
        functools.partial(_moe_b_kernel, layer=layer, tme=tme, pitch=pitch),
        out_shape=jax.ShapeDtypeStruct((nt * tme * pitch, LANES), U32),
        grid_spec=pltpu.PrefetchScalarGridSpec(
            num_scalar_prefetch=4, grid=(nt,),
            in_specs=[pl.BlockSpec((tme, f), lambda i, te, fr, nv, nx: (jnp.minimum(i, nv[0] - 1), 0)),
                      pl.BlockSpec(memory_space=pl.ANY)],
            out_specs=pl.BlockSpec((tme * pitch, LANES), lambda i, te, fr, nv, nx: (i, 0)),
            scratch_shapes=[pltpu.VMEM((f, d), BF16), pltpu.VMEM((f, d), F32), pltpu.SemaphoreType.DMA((1,))]),
        compiler_params=_params(("arbitrary",)),
        name="moe_down",
    )(plan["texp"], plan["first"], plan["nvalid"], plan["nxt"], hmid, w_down)


def _pos2d(length, d):
    rows = length // GRID_W
    r = jnp.repeat(jnp.arange(rows, dtype=F32), GRID_W)
    col = jnp.tile(jnp.arange(GRID_W, dtype=F32), rows)
    nf = d // 4
    omega = 1.0 / (10000.0 ** (jnp.arange(nf, dtype=F32) / nf))

    def enc(p):
        a = p[:, None] * omega[None, :]
        return jnp.concatenate([jnp.sin(a), jnp.cos(a)], axis=-1)

    return jnp.concatenate([enc(r), enc(col)], axis=-1)


def kernel(x_prompt, x_sample, state_ssm_re, state_ssm_im, c, c_ctx, w_ada, b_ada, ln_g, ln_b, s5_w_in, s5_lam_re, s5_lam_im, s5_log_dt, s5_b_re, s5_b_im, s5_c_re, s5_c_im, s5_d, s5_w_glu, s5_w_out, g_w_in, g_b_in, g_ln_g, g_ln_b, g_w_s, g_b_s, g_w_out, w_router, b_router, moe_w_gate, moe_w_up, moe_w_down):
    bp, seq, d = x_prompt.shape
    bs, ls, _ = x_sample.shape
    depth = w_ada.shape[0]
    n_exp = w_router.shape[1]
    tp, ts = bp * seq, bs * ls
    alpha = float((2 * depth) ** 0.25)
    tm = min(256, seq)
    tme = min(512, tp)
    pitch = _pitch(d // PAIR)
    assert bs + 1 <= COND_ROWS and d % PAIR == 0 and seq % tm == 0 and ls % tm == 0

    cond = jnp.concatenate([c_ctx[None].astype(F32), c.astype(F32),
                            jnp.zeros((COND_ROWS - 1 - bs, d), F32)], axis=0)
    mods = _adaln(cond, w_ada, b_ada)
    x, h = _embed(x_prompt.reshape(tp, d), x_sample.reshape(ts, d), _pos2d(ls, d), mods, tm, ls)
    wr, br = w_router.T, b_router.reshape(n_exp, 1)
    g_b_s_t = jnp.swapaxes(g_b_s, -1, -2)

    new_re, new_im = [], []
    for i in range(depth):
        j = i // 2
        if i % 2 == 0:
            wall, qt, a8 = _s5_prep(s5_lam_re[j], s5_lam_im[j], s5_log_dt[j], s5_b_re[j], s5_b_im[j],
                                    s5_c_re[j], s5_c_im[j])
            g, n = s5_lam_re.shape[2:]
            nsp = ts // min(4096, tp)
            h0 = jnp.stack([state_ssm_re[:, j, 0], state_ssm_im[:, j, 0], state_ssm_re[:, j, 1],
                            state_ssm_im[:, j, 1]]).reshape(4, nsp, bs // nsp, g * n).swapaxes(0, 1)
            u = _matmul(h, s5_w_in, (j,), name="s5_in")
            z, fin = _s5_scan(u, s5_d, j, wall, qt, a8, h0, tp, seq, ls, s5_b_re.shape[-1])
            fin = fin.swapaxes(0, 1).reshape(4, bp, g, n)
            new_re.append(jnp.stack([fin[0], fin[2]], axis=1))
            new_im.append(jnp.stack([fin[1], fin[3]], axis=1))
            tmm, tnn = min(MM_TM, tp + ts), min(MM_TN, d)
            zg = _matmul(z, s5_w_glu, (j,), epilogue=_epi_glu, extra=(z,),
                         extra_specs=(pl.BlockSpec((tmm, tnn), lambda b, a: (a, b)),),
                         out_dtype=BF16, name="s5_glu")
            ymix = _matmul(zg, s5_w_out, (j,), out_dtype=BF16, name="s5_out")
        else:
            tnn = min(MM_TN, 2 * d)
            zz = _matmul(h, g_w_in, (j,), epilogue=_epi_bias_gelu,
                         extra=(g_b_in.reshape(-1, 1, 2 * d),),
                         extra_specs=(pl.BlockSpec((None, 1, tnn), lambda b, a: (j, 0, b)),),
                         out_dtype=BF16, name="gmlp_in")
            gated = _gmlp_gate(zz, g_ln_g, g_ln_b, g_w_s, g_b_s_t, j)
            ymix = _matmul(gated, g_w_out, (j,), out_dtype=BF16, name="gmlp_out")
        x, hp, route = _postnorm_a(x, ymix, mods, i, ln_g, ln_b, wr, br, tm, tp, ls, alpha, pitch)
        plan = _plan(route, tme, n_exp)
        xs = _scatter_rows(hp.reshape(tp + ts, pitch, LANES), plan, tme, n_exp, tm)
        hmid = _moe_a(xs.reshape(-1, LANES), plan, moe_w_gate, moe_w_up, i, tme, pitch)
        ysp = _moe_b(hmid, plan, moe_w_down, i, tme, pitch)
        x, h = _postnorm_b(x, ysp.reshape(-1, pitch, LANES), plan["pos"], plan["cw"], mods, i, ln_g, ln_b,
                           tm, tp, ls, alpha, pitch, last=(i == depth - 1))
    return (x.reshape(bp, seq, d), h.reshape(bs, ls, d),
            jnp.stack(new_re, axis=1).astype(x_prompt.dtype), jnp.stack(new_im, axis=1).astype(x_prompt.dtype))
```
